```python
import math
import jax, jax.numpy as jnp
from jax import lax
import numpy as np

D_MODEL = 4096
BATCH = 4
SEQ = 4096
DEPTH = 4

GRID_W = 64
CTX_LEN = 256
HEAD_DIM = 128
N_HEADS_TOTAL = D_MODEL // HEAD_DIM
A_Q_HEADS = N_HEADS_TOTAL // 2
A_KV_HEADS = A_Q_HEADS // 4
A_GROUP = A_Q_HEADS // A_KV_HEADS
B_HEADS = N_HEADS_TOTAL // 4
NA_ROWS = 8
NA_COLS = 16
C_HEADS = N_HEADS_TOTAL // 4
C_QK_DIM = HEAD_DIM // 2
C_V_DIM = HEAD_DIM
A_Q_W = A_Q_HEADS * HEAD_DIM
A_KV_W = A_KV_HEADS * HEAD_DIM
B_W = B_HEADS * HEAD_DIM
C_QK_W = C_HEADS * 2 * C_QK_DIM
C_V_W = C_HEADS * C_V_DIM
GATE_RANK = 512
IN_W = A_Q_W + 2 * A_KV_W + 3 * B_W + 2 * C_QK_W + C_V_W + GATE_RANK
MIX_W = A_Q_W + B_W + C_V_W
N_BRANCH = 3
ADA_RANK = 256
D_FF = 256 * ((8 * D_MODEL + 3 * 256 - 1) // (3 * 256))
Q_BLOCK = 128
ROPE_THETA = 10000.0
EPS = 1e-6

kernel_name = "hybrid_gqa_natten_diffattn_dit_trunk"


def _rms_norm(x, g):
    xf = x.astype(jnp.float32)
    y = xf * lax.rsqrt(jnp.mean(xf * xf, axis=-1, keepdims=True) + EPS)
    return (y * g.astype(jnp.float32)).astype(x.dtype)


def _adaln(cond, w_down, w_up, b):
    mod = (jax.nn.silu(cond) @ w_down) @ w_up + b
    return jnp.split(mod, 6, axis=-1)


def _modulate(h, shift, scale):
    return h * (1 + scale[:, None, :]) + shift[:, None, :]


def _axial_rope_angles(n_tokens, dim):
    t = jnp.arange(n_tokens)
    rows = (t // GRID_W).astype(jnp.float32)
    cols = (t % GRID_W).astype(jnp.float32)
    n_pairs = dim // 4
    freqs = ROPE_THETA ** (-jnp.arange(n_pairs, dtype=jnp.float32) / n_pairs)
    return jnp.concatenate([rows[:, None] * freqs, cols[:, None] * freqs], axis=-1)


def _apply_rope(x, ang):
    half = x.shape[-1] // 2
    xf = x.astype(jnp.float32).reshape(x.shape[:-1] + (half, 2))
    bshape = (1, ang.shape[0]) + (1,) * (x.ndim - 3) + (half,)
    cos = jnp.cos(ang).reshape(bshape)
    sin = jnp.sin(ang).reshape(bshape)
    x0, x1 = xf[..., 0], xf[..., 1]
    out = jnp.stack([x0 * cos - x1 * sin, x0 * sin + x1 * cos], axis=-1)
    return out.reshape(x.shape).astype(x.dtype)


def _sweep_query_blocks(fn, q):
    b, s = q.shape[:2]
    nblk = s // Q_BLOCK
    qb = jnp.moveaxis(q.reshape((b, nblk, Q_BLOCK) + q.shape[2:]), 1, 0)
    ob = lax.map(fn, qb)
    return jnp.moveaxis(ob, 0, 1).reshape((b, s) + ob.shape[3:])


def _attend_gqa(q, k, v):
    s = jnp.einsum("bqhgd,bkhd->bhgqk", q, k).astype(jnp.float32) * (q.shape[-1] ** -0.5)
    p = jax.nn.softmax(s, axis=-1).astype(v.dtype)
    return jnp.einsum("bhgqk,bkhd->bqhgd", p, v)


def _attend_diff(q, k, v, lam):
    s = jnp.einsum("bqhcd,bkhcd->bhcqk", q, k).astype(jnp.float32) * (q.shape[-1] ** -0.5)
    p = jax.nn.softmax(s, axis=-1)
    w = (p[:, :, 0] - lam * p[:, :, 1]).astype(v.dtype)
    return jnp.einsum("bhqk,bkhe->bqhe", w, v)


def _neighbourhood_attention(q, k, v, k_ctx, v_ctx, rpb):
    b, s, h, d = q.shape
    rows = s // GRID_W
    kr = min(NA_ROWS, rows)
    c = np.arange(GRID_W)
    c0 = np.clip(c - NA_COLS // 2, 0, GRID_W - NA_COLS)
    col_mask = jnp.asarray((c[None, :] >= c0[:, None]) & (c[None, :] < c0[:, None] + NA_COLS))
    col_idx = jnp.asarray(np.clip(c[None, :] - c[:, None] + NA_COLS - 1, 0, 2 * NA_COLS - 2))
    q_grid = q.reshape(b, rows, GRID_W, h, d)
    k_grid = k.reshape(b, rows, GRID_W, h, d)
    v_grid = v.reshape(b, rows, GRID_W, h, d)
    scale = d ** -0.5

    def row_block(xs):
        r, q_row = xs
        r0 = jnp.clip(r - kr // 2, 0, rows - kr)
        k_blk = lax.dynamic_slice_in_dim(k_grid, r0, kr, axis=1)
        v_blk = lax.dynamic_slice_in_dim(v_grid, r0, kr, axis=1)
        s_loc = jnp.einsum("bqhd,biwhd->bhqiw", q_row, k_blk).astype(jnp.float32) * scale
        dr = r0 + jnp.arange(kr) - r + (NA_ROWS - 1)
        bias = jnp.transpose(rpb[:, dr][:, :, col_idx], (0, 2, 1, 3)).astype(jnp.float32)
        s_loc = jnp.where(col_mask[:, None, :], s_loc + bias, -jnp.inf)
        s_ctx = jnp.einsum("bqhd,bchd->bhqc", q_row, k_ctx).astype(jnp.float32) * scale
        s_all = jnp.concatenate([s_loc.reshape(b, h, GRID_W, kr * GRID_W), s_ctx], axis=-1)
        p = jax.nn.softmax(s_all, axis=-1).astype(v.dtype)
        p_loc = p[..., : kr * GRID_W].reshape(b, h, GRID_W, kr, GRID_W)
        p_ctx = p[..., kr * GRID_W:]
        return (jnp.einsum("bhqiw,biwhd->bqhd", p_loc, v_blk)
                + jnp.einsum("bhqc,bchd->bqhd", p_ctx, v_ctx))

    out = lax.map(row_block, (jnp.arange(rows), jnp.moveaxis(q_grid, 1, 0)))
    return jnp.moveaxis(out, 0, 1).reshape(b, s, h * d)


def _project(h, w):
    b, s = h.shape[:2]
    sizes = [A_Q_W, A_KV_W, A_KV_W, B_W, B_W, B_W, C_QK_W, C_QK_W, C_V_W, GATE_RANK]
    offsets = [int(o) for o in np.cumsum(sizes)[:-1]]
    a_q, a_k, a_v, b_q, b_k, b_v, c_q, c_k, c_v, g_low = jnp.split(h @ w, offsets, axis=-1)
    return (a_q.reshape(b, s, A_KV_HEADS, A_GROUP, HEAD_DIM),
            a_k.reshape(b, s, A_KV_HEADS, HEAD_DIM),
            a_v.reshape(b, s, A_KV_HEADS, HEAD_DIM),
            b_q.reshape(b, s, B_HEADS, HEAD_DIM),
            b_k.reshape(b, s, B_HEADS, HEAD_DIM),
            b_v.reshape(b, s, B_HEADS, HEAD_DIM),
            c_q.reshape(b, s, C_HEADS, 2, C_QK_DIM),
            c_k.reshape(b, s, C_HEADS, 2, C_QK_DIM),
            c_v.reshape(b, s, C_HEADS, C_V_DIM),
            g_low)


def _merge(g_low, o_a, o_b, o_c, w_gate_up, b_gate, w_branch, w_out):
    gates = jax.nn.sigmoid((g_low @ w_gate_up + b_gate).astype(jnp.float32)).astype(o_a.dtype)
    g_a, g_b, g_c = jnp.split(gates, N_BRANCH, axis=-1)
    merged = (g_a * (o_a @ w_branch[:A_Q_W])
              + g_b * (o_b @ w_branch[A_Q_W:A_Q_W + B_W])
              + g_c * (o_c @ w_branch[A_Q_W + B_W:]))
    return merged @ w_out


def _swiglu(h, w_in, w_out):
    gate, up = jnp.split(h @ w_in, 2, axis=-1)
    return (jax.nn.silu(gate) * up) @ w_out


def setup_inputs(seed: int = 0) -> dict:
    key = jax.random.key(seed)
    ks = jax.random.split(key, 26)
    f32 = jnp.float32
    L, D = DEPTH, D_MODEL

    def nrm(k, shape, scale):
        return jax.random.normal(k, shape, f32) * scale

    def gain(k, shape):
        return 1.0 + 0.02 * jax.random.normal(k, shape, f32)

    return {
        "x": nrm(ks[0], (BATCH, SEQ, D), 1.0),
        "c": nrm(ks[1], (BATCH, D), 1.0),
        "ctx": nrm(ks[2], (BATCH, CTX_LEN, D), 1.0),
        "c_ctx": nrm(ks[3], (D,), 1.0),
        "ada_down": nrm(ks[4], (L, D, ADA_RANK), D ** -0.5),
        "ada_up": nrm(ks[5], (L, ADA_RANK, 6 * D), ADA_RANK ** -0.5),
        "ada_bias": nrm(ks[6], (L, 6 * D), 0.02),
        "norm_mix_pre": gain(ks[7], (L, D)),
        "norm_mix_post": gain(ks[8], (L, D)),
        "norm_ffn_pre": gain(ks[9], (L, D)),
        "norm_ffn_post": gain(ks[10], (L, D)),
        "w_in": nrm(ks[11], (L, D, IN_W), D ** -0.5),
        "a_q_norm": gain(ks[12], (L, HEAD_DIM)),
        "a_k_norm": gain(ks[13], (L, HEAD_DIM)),
        "b_rel_bias": nrm(ks[14], (L, B_HEADS, 2 * NA_ROWS - 1, 2 * NA_COLS - 1), 0.1),
        "c_lambda_q1": nrm(ks[15], (L, C_QK_DIM), 0.1),
        "c_lambda_k1": nrm(ks[16], (L, C_QK_DIM), 0.1),
        "c_lambda_q2": nrm(ks[17], (L, C_QK_DIM), 0.1),
        "c_lambda_k2": nrm(ks[18], (L, C_QK_DIM), 0.1),
        "c_subln": gain(ks[19], (L, C_V_DIM)),
        "w_gate_up": nrm(ks[20], (L, GATE_RANK, N_BRANCH * D), GATE_RANK ** -0.5),
        "b_gate": nrm(ks[21], (L, N_BRANCH * D), 0.02),
        "w_branch": nrm(ks[22], (L, MIX_W, D), MIX_W ** -0.5),
        "w_out": nrm(ks[23], (L, D, D), D ** -0.5),
        "w_ffn_in": nrm(ks[24], (L, D, 2 * D_FF), D ** -0.5),
        "w_ffn_out": nrm(ks[25], (L, D_FF, D), D_FF ** -0.5),
    }


def reference(x, c, ctx, c_ctx, ada_down, ada_up, ada_bias, norm_mix_pre, norm_mix_post,
              norm_ffn_pre, norm_ffn_post, w_in, a_q_norm, a_k_norm, b_rel_bias,
              c_lambda_q1, c_lambda_k1, c_lambda_q2, c_lambda_k2, c_subln,
              w_gate_up, b_gate, w_branch, w_out, w_ffn_in, w_ffn_out):
    b, s = x.shape[:2]
    n_ctx = ctx.shape[1]
    ang_a = _axial_rope_angles(s, HEAD_DIM)
    ang_c = _axial_rope_angles(s, C_QK_DIM)
    for l in range(DEPTH):
        last = l == DEPTH - 1
        sh1_x, sc1_x, g1_x, sh2_x, sc2_x, g2_x = _adaln(c, ada_down[l], ada_up[l], ada_bias[l])
        sh1_c, sc1_c, g1_c, sh2_c, sc2_c, g2_c = _adaln(c_ctx[None], ada_down[l], ada_up[l], ada_bias[l])

        h_x = _modulate(_rms_norm(x, norm_mix_pre[l]), sh1_x, sc1_x)
        h_c = _modulate(_rms_norm(ctx, norm_mix_pre[l]), sh1_c, sc1_c)
        aq_x, ak_x, av_x, bq_x, bk_x, bv_x, cq_x, ck_x, cv_x, gl_x = _project(h_x, w_in[l])
        aq_c, ak_c, av_c, bq_c, bk_c, bv_c, cq_c, ck_c, cv_c, gl_c = _project(h_c, w_in[l])

        aq_x = _apply_rope(_rms_norm(aq_x, a_q_norm[l]), ang_a)
        ak_x = _apply_rope(_rms_norm(ak_x, a_k_norm[l]), ang_a)
        aq_c = _rms_norm(aq_c, a_q_norm[l])
        ak_c = _rms_norm(ak_c, a_k_norm[l])
        k_all_a = jnp.concatenate([ak_c, ak_x], axis=1)
        v_all_a = jnp.concatenate([av_c, av_x], axis=1)
        oa_x = _sweep_query_blocks(lambda qb: _attend_gqa(qb, k_all_a, v_all_a), aq_x).reshape(b, s, A_Q_W)

        ob_x = _neighbourhood_attention(bq_x, bk_x, bv_x, bk_c, bv_c, b_rel_bias[l])

        lam_init = 0.8 - 0.6 * math.exp(-0.3 * l)
        lam = (jnp.exp(jnp.sum(c_lambda_q1[l].astype(jnp.float32) * c_lambda_k1[l].astype(jnp.float32)))
               - jnp.exp(jnp.sum(c_lambda_q2[l].astype(jnp.float32) * c_lambda_k2[l].astype(jnp.float32)))
               + lam_init)
        cq_x = _apply_rope(cq_x, ang_c)
        ck_x = _apply_rope(ck_x, ang_c)
        k_all_c = jnp.concatenate([ck_c, ck_x], axis=1)
        v_all_c = jnp.concatenate([cv_c, cv_x], axis=1)
        oc_x = _sweep_query_blocks(lambda qb: _attend_diff(qb, k_all_c, v_all_c, lam), cq_x)
        oc_x = (_rms_norm(oc_x, c_subln[l]) * (1 - lam_init)).reshape(b, s, C_V_W)

        mix_x = _merge(gl_x, oa_x, ob_x, oc_x, w_gate_up[l], b_gate[l], w_branch[l], w_out[l])
        x = x + g1_x[:, None, :] * _rms_norm(mix_x, norm_mix_post[l])

        f_x = _swiglu(_modulate(_rms_norm(x, norm_ffn_pre[l]), sh2_x, sc2_x), w_ffn_in[l], w_ffn_out[l])
        x = x + g2_x[:, None, :] * _rms_norm(f_x, norm_ffn_post[l])

        if not last:
            oa_c = _attend_gqa(aq_c, ak_c, av_c).reshape(b, n_ctx, A_Q_W)
            ob_c = _attend_gqa(bq_c[:, :, :, None], bk_c, bv_c).reshape(b, n_ctx, B_W)
            oc_c = (_rms_norm(_attend_diff(cq_c, ck_c, cv_c, lam), c_subln[l]) * (1 - lam_init)).reshape(b, n_ctx, C_V_W)
            mix_c = _merge(gl_c, oa_c, ob_c, oc_c, w_gate_up[l], b_gate[l], w_branch[l], w_out[l])
            ctx = ctx + g1_c[:, None, :] * _rms_norm(mix_c, norm_mix_post[l])
            f_c = _swiglu(_modulate(_rms_norm(ctx, norm_ffn_pre[l]), sh2_c, sc2_c), w_ffn_in[l], w_ffn_out[l])
            ctx = ctx + g2_c[:, None, :] * _rms_norm(f_c, norm_ffn_post[l])
    return x
```

```python
import functools
import math

import jax
import jax.numpy as jnp
import numpy as np
from jax import lax
from jax.experimental import pallas as pl
from jax.experimental.pallas import tpu as pltpu

HEAD_DIM = 128
GRID_W = 64
NA_ROWS = 8
NA_COLS = 16
Q_GROUP = 4
ROPE_THETA = 10000.0
EPS = 1e-6
ROW_TILE = 256
MM_TILE_M = 1024
VMEM_LIMIT = 56 * 1024 * 1024
BF16 = jnp.bfloat16
F32 = jnp.float32


def _params(*sem):
    return pltpu.CompilerParams(dimension_semantics=sem, vmem_limit_bytes=VMEM_LIMIT)


def _dot(a, b):
    return jnp.dot(a, b, preferred_element_type=F32)


def _dot_nt(a, b):
    return lax.dot_general(a, b, (((1,), (1,)), ((), ())), preferred_element_type=F32)


def _rms(x, gain):
    return x * lax.rsqrt(jnp.mean(x * x, axis=-1, keepdims=True) + EPS) * gain


def _adaln_kernel(cond_ref, down_ref, up_ref, bias_ref, o_ref):
    cond = cond_ref[...]
    act = (cond * jax.nn.sigmoid(cond)).astype(BF16)
    low = _dot(act, down_ref[0].astype(BF16))
    o_ref[0, 0] = _dot(low.astype(BF16), up_ref[0].astype(BF16)) + bias_ref[0]


def _adaln(cond, ada_down, ada_up, ada_bias):
    depth, d, rank = ada_down.shape
    rows = cond.shape[0]
    return pl.pallas_call(
        _adaln_kernel,
        grid=(depth, 6),
        in_specs=[
            pl.BlockSpec((rows, d), lambda l, k: (0, 0)),
            pl.BlockSpec((1, d, rank), lambda l, k: (l, 0, 0)),
            pl.BlockSpec((1, rank, d), lambda l, k: (l, 0, k)),
            pl.BlockSpec((1, 1, d), lambda l, k: (l, 0, k)),
        ],
        out_specs=pl.BlockSpec((1, 1, rows, d), lambda l, k: (l, k, 0, 0)),
        out_shape=jax.ShapeDtypeStruct((depth, 6, rows, d), F32),
        compiler_params=_params("arbitrary", "arbitrary"),
        name="adaln",
    )(cond, ada_down, ada_up, ada_bias.reshape(depth, 1, 6 * d))


def _mod_row(i, tiles_per_batch, ctx_tiles, batch):
    return jnp.where(i % tiles_per_batch < ctx_tiles, batch, i // tiles_per_batch)


def _prenorm_kernel(x_ref, gain_ref, shift_ref, scale_ref, h_ref):
    y = _rms(x_ref[...], gain_ref[...])
    h_ref[...] = (y * (1.0 + scale_ref[0, 0, 0]) + shift_ref[0, 0, 0]).astype(BF16)


def _prenorm(xs, gain, mods, layer, k_shift, k_scale, geom):
    m, d = xs.shape
    tpb, ctx_tiles, batch = geom
    row = functools.partial(_mod_row, tiles_per_batch=tpb, ctx_tiles=ctx_tiles, batch=batch)
    return pl.pallas_call(
        _prenorm_kernel,
        grid=(m // ROW_TILE,),
        in_specs=[
            pl.BlockSpec((ROW_TILE, d), lambda i: (i, 0)),
            pl.BlockSpec((1, d), lambda i: (0, 0)),
            pl.BlockSpec((1, 1, 1, 1, d), lambda i: (layer, k_shift, row(i), 0, 0)),
            pl.BlockSpec((1, 1, 1, 1, d), lambda i: (layer, k_scale, row(i), 0, 0)),
        ],
        out_specs=pl.BlockSpec((ROW_TILE, d), lambda i: (i, 0)),
        out_shape=jax.ShapeDtypeStruct((m, d), BF16),
        compiler_params=_params("arbitrary"),
        name="prenorm",
    )(xs, gain.reshape(1, d), mods, mods)


def _resid_kernel(x_ref, y_ref, post_ref, gate_ref, pre_ref, shift_ref, scale_ref, xo_ref, h_ref):
    x_new = x_ref[...] + gate_ref[0, 0, 0] * _rms(y_ref[...], post_ref[...])
    xo_ref[...] = x_new
    h = _rms(x_new, pre_ref[...])
    h_ref[...] = (h * (1.0 + scale_ref[0, 0, 0]) + shift_ref[0, 0, 0]).astype(BF16)


def _resid(xs, y, post_gain, pre_gain, mods, layer, k_gate, mod_layer, k_shift, k_scale, geom):
    m, d = xs.shape
    tpb, ctx_tiles, batch = geom
    row = functools.partial(_mod_row, tiles_per_batch=tpb, ctx_tiles=ctx_tiles, batch=batch)
    tile = pl.BlockSpec((ROW_TILE, d), lambda i: (i, 0))
    vec = pl.BlockSpec((1, d), lambda i: (0, 0))
    return pl.pallas_call(
        _resid_kernel,
        grid=(m // ROW_TILE,),
        in_specs=[
            tile, tile, vec,
            pl.BlockSpec((1, 1, 1, 1, d), lambda i: (layer, k_gate, row(i), 0, 0)),
            vec,
            pl.BlockSpec((1, 1, 1, 1, d), lambda i: (mod_layer, k_shift, row(i), 0, 0)),
            pl.BlockSpec((1, 1, 1, 1, d), lambda i: (mod_layer, k_scale, row(i), 0, 0)),
        ],
        out_specs=[tile, tile],
        out_shape=[jax.ShapeDtypeStruct((m, d), F32), jax.ShapeDtypeStruct((m, d), BF16)],
        compiler_params=_params("arbitrary"),
        name="resid",
    )(xs, y, post_gain.reshape(1, d), mods, pre_gain.reshape(1, d), mods, mods)


def _rope(y, cos, sin):
    return y * cos + pltpu.roll(y, HEAD_DIM // 2, 1) * sin


def _inproj_kernel(h_ref, w_ref, qg_ref, kg_ref, cosa_ref, sina_ref, cosc_ref, sinc_ref, o_ref,
                   *, tn, q_tiles, k_tiles, rope_c_ranges):
    j = pl.program_id(1)
    acc = _dot(h_ref[...], w_ref[...])
    heads = tn // HEAD_DIM

    def store(fn):
        for hh in range(heads):
            sl = slice(hh * HEAD_DIM, (hh + 1) * HEAD_DIM)
            o_ref[:, sl] = fn(acc[:, sl]).astype(BF16)

    is_q = j < q_tiles
    is_k = jnp.logical_and(j >= q_tiles, j < q_tiles + k_tiles)
    is_c = functools.reduce(jnp.logical_or,
                            [jnp.logical_and(j >= lo, j < hi) for lo, hi in rope_c_ranges])
    plain = jnp.logical_not(jnp.logical_or(jnp.logical_or(is_q, is_k), is_c))

    @pl.when(is_q)
    def _():
        store(lambda a: _rope(_rms(a, qg_ref[...]), cosa_ref[...], sina_ref[...]))

    @pl.when(is_k)
    def _():
        store(lambda a: _rope(_rms(a, kg_ref[...]), cosa_ref[...], sina_ref[...]))

    @pl.when(is_c)
    def _():
        store(lambda a: _rope(a, cosc_ref[...], sinc_ref[...]))

    @pl.when(plain)
    def _():
        o_ref[...] = acc.astype(BF16)


def _inproj(h, w, q_gain, k_gain, tabs, cfg):
    m, d = h.shape
    n = w.shape[1]
    tn, tm = cfg["in_tn"], cfg["mm_tm"]
    tab = pl.BlockSpec((tm, HEAD_DIM), lambda i, j: (i, 0))
    vec = pl.BlockSpec((1, HEAD_DIM), lambda i, j: (0, 0))
    kern = functools.partial(
        _inproj_kernel, tn=tn, q_tiles=cfg["a_q_w"] // tn, k_tiles=cfg["a_kv_w"] // tn,
        rope_c_ranges=[(cfg["off_cq"] // tn, (cfg["off_cq"] + 2 * cfg["c_qk_w"]) // tn)])
    return pl.pallas_call(
        kern,
        grid=(m // tm, n // tn),
        in_specs=[
            pl.BlockSpec((tm, d), lambda i, j: (i, 0)),
            pl.BlockSpec((d, tn), lambda i, j: (0, j)),
            vec, vec, tab, tab, tab, tab,
        ],
        out_specs=pl.BlockSpec((tm, tn), lambda i, j: (i, j)),
        out_shape=jax.ShapeDtypeStruct((m, n), BF16),
        compiler_params=_params("arbitrary", "arbitrary"),
        name="inproj",
    )(h, w, q_gain.reshape(1, HEAD_DIM), k_gain.reshape(1, HEAD_DIM), *tabs)


def _softmax_pv(s, v):
    m = jnp.max(s, axis=-1, keepdims=True)
    p = jnp.exp(s - m)
    l = jnp.sum(p, axis=-1, keepdims=True)
    return _dot(p.astype(BF16), v) / l


def _attn_a_kernel(q_ref, k_ref, v_ref, o_ref, *, n_ctx, ctx_tiles, group, scale):
    qi = pl.program_id(2)

    def run(k, v):
        for g in range(group):
            sl = slice(g * HEAD_DIM, (g + 1) * HEAD_DIM)
            s = _dot_nt(q_ref[0, :, sl], k) * scale
            o_ref[0, :, sl] = _softmax_pv(s, v).astype(BF16)

    @pl.when(qi < ctx_tiles)
    def _():
        run(k_ref[0, :n_ctx], v_ref[0, :n_ctx])

    @pl.when(qi >= ctx_tiles)
    def _():
        run(k_ref[0], v_ref[0])


def _attn_a(proj, cfg):
    b, t, _ = proj.shape
    gw = Q_GROUP * HEAD_DIM
    kvh = cfg["a_kv_w"] // HEAD_DIM
    k_blk = cfg["a_q_w"] // HEAD_DIM
    v_blk = (cfg["a_q_w"] + cfg["a_kv_w"]) // HEAD_DIM
    kern = functools.partial(_attn_a_kernel, n_ctx=cfg["n_ctx"], ctx_tiles=cfg["n_ctx"] // ROW_TILE,
                             group=Q_GROUP, scale=HEAD_DIM ** -0.5)
    return pl.pallas_call(
        kern,
        grid=(b, kvh, t // ROW_TILE),
        in_specs=[
            pl.BlockSpec((1, ROW_TILE, gw), lambda bi, hi, qi: (bi, qi, hi)),
            pl.BlockSpec((1, t, HEAD_DIM), lambda bi, hi, qi: (bi, 0, k_blk + hi)),
            pl.BlockSpec((1, t, HEAD_DIM), lambda bi, hi, qi: (bi, 0, v_blk + hi)),
        ],
        out_specs=pl.BlockSpec((1, ROW_TILE, gw), lambda bi, hi, qi: (bi, qi, hi)),
        out_shape=jax.ShapeDtypeStruct((b, t, cfg["a_q_w"]), BF16),
        compiler_params=_params("arbitrary", "arbitrary", "arbitrary"),
        name="attn_a",
    )(proj, proj, proj)


def _attn_c_kernel(q_ref, k_ref, v_ref, l1_ref, l2_ref, l3_ref, l4_ref, gain_ref, o_ref,
                   *, n_ctx, ctx_tiles, scale, lam_init):
    qi = pl.program_id(2)
    lam = (jnp.exp(jnp.sum(l1_ref[...] * l2_ref[...], axis=-1, keepdims=True))
           - jnp.exp(jnp.sum(l3_ref[...] * l4_ref[...], axis=-1, keepdims=True)) + lam_init)
    q = q_ref[0]
    lane = lax.broadcasted_iota(jnp.int32, q.shape, 1)
    first = (lane % (HEAD_DIM // 2)) < (HEAD_DIM // 4)
    q1 = jnp.where(first, q, jnp.zeros_like(q))
    q2 = jnp.where(first, jnp.zeros_like(q), q)

    def run(k, v):
        o1 = _softmax_pv(_dot_nt(q1, k) * scale, v)
        o2 = _softmax_pv(_dot_nt(q2, k) * scale, v)
        o = _rms(o1 - lam * o2, gain_ref[...]) * (1.0 - lam_init)
        o_ref[0] = o.astype(BF16)

    @pl.when(qi < ctx_tiles)
    def _():
        run(k_ref[0, :n_ctx], v_ref[0, :n_ctx])

    @pl.when(qi >= ctx_tiles)
    def _():
        run(k_ref[0], v_ref[0])


def _attn_c(proj, lams, subln, lam_init, cfg):
    b, t, _ = proj.shape
    heads = cfg["c_v_w"] // HEAD_DIM
    q_blk = cfg["off_cq"] // HEAD_DIM
    k_blk = q_blk + cfg["c_qk_w"] // HEAD_DIM
    v_blk = k_blk + cfg["c_qk_w"] // HEAD_DIM
    qk_dim = HEAD_DIM // 2
    lam_spec = pl.BlockSpec((1, qk_dim), lambda bi, hi, qi: (0, 0))
    kern = functools.partial(_attn_c_kernel, n_ctx=cfg["n_ctx"], ctx_tiles=cfg["n_ctx"] // ROW_TILE,
                             scale=qk_dim ** -0.5, lam_init=lam_init)
    return pl.pallas_call(
        kern,
        grid=(b, heads, t // ROW_TILE),
        in_specs=[
            pl.BlockSpec((1, ROW_TILE, HEAD_DIM), lambda bi, hi, qi: (bi, qi, q_blk + hi)),
            pl.BlockSpec((1, t, HEAD_DIM), lambda bi, hi, qi: (bi, 0, k_blk + hi)),
            pl.BlockSpec((1, t, HEAD_DIM), lambda bi, hi, qi: (bi, 0, v_blk + hi)),
            lam_spec, lam_spec, lam_spec, lam_spec,
            pl.BlockSpec((1, HEAD_DIM), lambda bi, hi, qi: (0, 0)),
        ],
        out_specs=pl.BlockSpec((1, ROW_TILE, HEAD_DIM), lambda bi, hi, qi: (bi, qi, hi)),
        out_shape=jax.ShapeDtypeStruct((b, t, cfg["c_v_w"]), BF16),
        compiler_params=_params("arbitrary", "arbitrary", "arbitrary"),
        name="attn_c",
    )(proj, proj, proj, *[v.reshape(1, qk_dim) for v in lams], subln.reshape(1, HEAD_DIM))


def _attn_b_kernel(q_ref, k_ref, v_ref, bias_ref, o_ref, *, n_ctx, blocks, q_rows, k_rows, scale):
    k_ctx = k_ref[0, :n_ctx]
    v_ctx = v_ref[0, :n_ctx]
    o_ref[0, :n_ctx] = _softmax_pv(_dot_nt(q_ref[0, :n_ctx], k_ctx) * scale, v_ctx).astype(BF16)
    for q0, k0, case in blocks:
        q = q_ref[0, n_ctx + q0:n_ctx + q0 + q_rows]
        k_win = k_ref[0, n_ctx + k0:n_ctx + k0 + k_rows]
        v_win = v_ref[0, n_ctx + k0:n_ctx + k0 + k_rows]
        s_loc = _dot_nt(q, k_win) * scale + bias_ref[0, case]
        s_ctx = _dot_nt(q, k_ctx) * scale
        m = jnp.maximum(jnp.max(s_loc, axis=-1, keepdims=True), jnp.max(s_ctx, axis=-1, keepdims=True))
        p_loc = jnp.exp(s_loc - m)
        p_ctx = jnp.exp(s_ctx - m)
        l = jnp.sum(p_loc, axis=-1, keepdims=True) + jnp.sum(p_ctx, axis=-1, keepdims=True)
        o = (_dot(p_loc.astype(BF16), v_win) + _dot(p_ctx.astype(BF16), v_ctx)) / l
        o_ref[0, n_ctx + q0:n_ctx + q0 + q_rows] = o.astype(BF16)


def _nbr_plan(seq):
    rows = seq // GRID_W
    kr = min(NA_ROWS, rows)
    q_rows_n = min(NA_ROWS, rows)
    win = min(2 * NA_ROWS, rows)
    cases, blocks = [], []
    for i in range(rows // q_rows_n):
        ws = int(np.clip(i * q_rows_n - kr // 2, 0, rows - win))
        drmap = np.full((q_rows_n, win), -1, np.int64)
        for a in range(q_rows_n):
            r = i * q_rows_n + a
            r0 = int(np.clip(r - kr // 2, 0, rows - kr))
            for w in range(win):
                kr_abs = ws + w
                if r0 <= kr_abs < r0 + kr:
                    drmap[a, w] = kr_abs - r + (NA_ROWS - 1)
        key = drmap.tobytes()
        keys = [c.tobytes() for c in cases]
        if key not in keys:
            cases.append(drmap)
            keys.append(key)
        blocks.append((i * q_rows_n * GRID_W, ws * GRID_W, keys.index(key)))
    return blocks, np.stack(cases), q_rows_n * GRID_W, win * GRID_W


def _nbr_bias(rpb, case_maps):
    depth, heads = rpb.shape[:2]
    c = np.arange(GRID_W)
    c0 = np.clip(c - NA_COLS // 2, 0, GRID_W - NA_COLS)
    col_mask = (c[None, :] >= c0[:, None]) & (c[None, :] < c0[:, None] + NA_COLS)
    col_idx = np.clip(c[None, :] - c[:, None] + NA_COLS - 1, 0, 2 * NA_COLS - 2)
    toe = jnp.where(col_mask, rpb.astype(F32)[:, :, :, col_idx], -jnp.inf)
    toe = jnp.concatenate([toe, jnp.full((depth, heads, 1, GRID_W, GRID_W), -jnp.inf, F32)], axis=2)
    n_case, qr, win = case_maps.shape
    idx = np.where(case_maps < 0, 2 * NA_ROWS - 1, case_maps).reshape(-1)
    blk = jnp.take(toe, idx, axis=2).reshape(depth, heads, n_case, qr, win, GRID_W, GRID_W)
    return jnp.transpose(blk, (0, 1, 2, 3, 5, 4, 6)).reshape(depth, heads, n_case, qr * GRID_W, win * GRID_W)


def _attn_b(proj, bias, layer, plan, cfg):
    b, t, _ = proj.shape
    heads = cfg["b_w"] // HEAD_DIM
    q_blk = cfg["off_bq"] // HEAD_DIM
    k_blk = q_blk + heads
    v_blk = k_blk + heads
    blocks, _, q_rows, k_rows = plan
    n_case = bias.shape[2]
    kern = functools.partial(_attn_b_kernel, n_ctx=cfg["n_ctx"], blocks=blocks, q_rows=q_rows,
                             k_rows=k_rows, scale=HEAD_DIM ** -0.5)
    return pl.pallas_call(
        kern,
        grid=(heads, b),
        in_specs=[
            pl.BlockSpec((1, t, HEAD_DIM), lambda hi, bi: (bi, 0, q_blk + hi)),
            pl.BlockSpec((1, t, HEAD_DIM), lambda hi, bi: (bi, 0, k_blk + hi)),
            pl.BlockSpec((1, t, HEAD_DIM), lambda hi, bi: (bi, 0, v_blk + hi)),
            pl.BlockSpec((None, 1, n_case, q_rows, k_rows), lambda hi, bi: (layer, hi, 0, 0, 0)),
        ],
        out_specs=pl.BlockSpec((1, t, HEAD_DIM), lambda hi, bi: (bi, 0, hi)),
        out_shape=jax.ShapeDtypeStruct((b, t, cfg["b_w"]), BF16),
        compiler_params=_params("arbitrary", "arbitrary"),
        name="attn_b",
    )(proj, proj, proj, bias)


def _merge_kernel(g_ref, oa_ref, ob_ref, oc_ref, wg_ref, bg_ref, wa_ref, wb_ref, wc_ref, o_ref):
    g_low = g_ref[...]
    acc = None
    for br, (o_br, w_br) in enumerate(((oa_ref, wa_ref), (ob_ref, wb_ref), (oc_ref, wc_ref))):
        gate = jax.nn.sigmoid(_dot(g_low, wg_ref[br]) + bg_ref[br])
        term = gate * _dot(o_br[...], w_br[...])
        acc = term if acc is None else acc + term
    o_ref[...] = acc.astype(BF16)


def _merge(proj2, o_a, o_b, o_c, w_gate, b_gate, w_branch, cfg):
    m = proj2.shape[0]
    d = w_branch.shape[1]
    rank = w_gate.shape[1]
    tm, tn = cfg["mm_tm"], cfg["mm_tn"]
    a_w, b_w, c_w = cfg["a_q_w"], cfg["b_w"], cfg["c_v_w"]
    g_blk = cfg["off_g"] // rank
    return pl.pallas_call(
        _merge_kernel,
        grid=(m // tm, d // tn),
        in_specs=[
            pl.BlockSpec((tm, rank), lambda i, j: (i, g_blk)),
            pl.BlockSpec((tm, a_w), lambda i, j: (i, 0)),
            pl.BlockSpec((tm, b_w), lambda i, j: (i, 0)),
            pl.BlockSpec((tm, c_w), lambda i, j: (i, 0)),
            pl.BlockSpec((3, rank, tn), lambda i, j: (0, 0, j)),
            pl.BlockSpec((3, 1, tn), lambda i, j: (0, 0, j)),
            pl.BlockSpec((a_w, tn), lambda i, j: (0, j)),
            pl.BlockSpec((b_w, tn), lambda i, j: (a_w // b_w, j)),
            pl.BlockSpec((c_w, tn), lambda i, j: ((a_w + b_w) // c_w, j)),
        ],
        out_specs=pl.BlockSpec((tm, tn), lambda i, j: (i, j)),
        out_shape=jax.ShapeDtypeStruct((m, d), BF16),
        compiler_params=_params("arbitrary", "arbitrary"),
        name="merge",
    )(proj2, o_a, o_b, o_c, w_gate, b_gate, w_branch, w_branch, w_branch)


def _matmul_kernel(a_ref, w_ref, o_ref):
    o_ref[...] = _dot(a_ref[...], w_ref[...])


def _matmul_f32(a, w, tm, tn, name):
    m, k = a.shape
    n = w.shape[1]
    return pl.pallas_call(
        _matmul_kernel,
        grid=(m // tm, n // tn),
        in_specs=[pl.BlockSpec((tm, k), lambda i, j: (i, 0)),
                  pl.BlockSpec((k, tn), lambda i, j: (0, j))],
        out_specs=pl.BlockSpec((tm, tn), lambda i, j: (i, j)),
        out_shape=jax.ShapeDtypeStruct((m, n), F32),
        compiler_params=_params("arbitrary", "arbitrary"),
        name=name,
    )(a, w)


def _ffn_in_kernel(h_ref, wg_ref, wu_ref, o_ref):
    h = h_ref[...]
    gate = _dot(h, wg_ref[...])
    up = _dot(h, wu_ref[...])
    o_ref[...] = (gate * jax.nn.sigmoid(gate) * up).astype(BF16)


def _ffn_in(h, w, cfg):
    m, d = h.shape
    d_ff = w.shape[1] // 2
    tm, tn = cfg["mm_tm"], cfg["ffn_tn"]
    up_blk = d_ff // tn
    return pl.pallas_call(
        _ffn_in_kernel,
        grid=(m // tm, d_ff // tn),
        in_specs=[pl.BlockSpec((tm, d), lambda i, j: (i, 0)),
                  pl.BlockSpec((d, tn), lambda i, j: (0, j)),
                  pl.BlockSpec((d, tn), lambda i, j: (0, up_blk + j))],
        out_specs=pl.BlockSpec((tm, tn), lambda i, j: (i, j)),
        out_shape=jax.ShapeDtypeStruct((m, d_ff), BF16),
        compiler_params=_params("arbitrary", "arbitrary"),
        name="ffn_in",
    )(h, w, w)


def _rope_tables(seq, n_ctx, batch, dim, comps):
    t = np.arange(seq)
    rows = (t // GRID_W).astype(np.float32)
    cols = (t % GRID_W).astype(np.float32)
    n_pairs = dim // 4
    freqs = jnp.asarray(ROPE_THETA, F32) ** (-jnp.arange(n_pairs, dtype=F32) / n_pairs)
    ang = jnp.concatenate([jnp.asarray(rows)[:, None] * freqs, jnp.asarray(cols)[:, None] * freqs], axis=-1)
    cos = jnp.tile(jnp.cos(ang), (1, 2 * comps))
    sin = jnp.tile(jnp.sin(ang), (1, comps))
    sin = jnp.concatenate([-sin, sin], axis=-1)
    cos = jnp.concatenate([jnp.ones((n_ctx, HEAD_DIM), F32), cos], axis=0)
    sin = jnp.concatenate([jnp.zeros((n_ctx, HEAD_DIM), F32), sin], axis=0)
    return jnp.tile(cos, (batch, 1)), jnp.tile(sin, (batch, 1))


def _head_perm(comps):
    pairs = HEAD_DIM // (2 * comps)
    perm = np.zeros(HEAD_DIM, np.int64)
    for e in range(2):
        for c in range(comps):
            for p in range(pairs):
                perm[e * (HEAD_DIM // 2) + c * pairs + p] = c * (HEAD_DIM // comps) + 2 * p + e
    return perm


def _config(d, n_ctx, in_w, rank, d_ff):
    heads = d // HEAD_DIM
    a_q_w = (heads // 2) * HEAD_DIM
    a_kv_w = a_q_w // Q_GROUP
    b_w = (heads // 4) * HEAD_DIM
    c_qk_w = c_v_w = (heads // 4) * HEAD_DIM
    off_bq = a_q_w + 2 * a_kv_w
    off_cq = off_bq + 3 * b_w
    off_g = off_cq + 2 * c_qk_w + c_v_w
    assert off_g + rank == in_w and off_g % rank == 0
    assert a_q_w % b_w == 0 and (a_q_w + b_w) % c_v_w == 0
    in_tn = math.gcd(math.gcd(a_kv_w, b_w), 512)
    ffn_tn = math.gcd(d_ff, 512)
    return dict(n_ctx=n_ctx, a_q_w=a_q_w, a_kv_w=a_kv_w, b_w=b_w, c_qk_w=c_qk_w, c_v_w=c_v_w,
                off_bq=off_bq, off_cq=off_cq, off_g=off_g, in_tn=in_tn, ffn_tn=ffn_tn,
                mm_tn=min(512, d))


def kernel(x, c, ctx, c_ctx, ada_down, ada_up, ada_bias, norm_mix_pre, norm_mix_post, norm_ffn_pre,
           norm_ffn_post, w_in, a_q_norm, a_k_norm, b_rel_bias, c_lambda_q1, c_lambda_k1, c_lambda_q2,
           c_lambda_k2, c_subln, w_gate_up, b_gate, w_branch, w_out, w_ffn_in, w_ffn_out):
    batch, seq, d = x.shape
    n_ctx = ctx.shape[1]
    depth, _, in_w = w_in.shape
    rank = w_gate_up.shape[1]
    d_ff = w_ffn_out.shape[1]
    t = n_ctx + seq
    m = batch * t
    cfg = _config(d, n_ctx, in_w, rank, d_ff)
    assert n_ctx % ROW_TILE == 0 and seq % ROW_TILE == 0 and seq % (NA_ROWS * GRID_W) == 0
    cfg["mm_tm"] = math.gcd(m, MM_TILE_M)
    geom = (t // ROW_TILE, n_ctx // ROW_TILE, batch)

    perm = np.arange(in_w)
    for off, width, comps in ((0, cfg["a_q_w"] + cfg["a_kv_w"], 1), (cfg["off_cq"], 2 * cfg["c_qk_w"], 2)):
        for hstart in range(off, off + width, HEAD_DIM):
            perm[hstart:hstart + HEAD_DIM] = hstart + _head_perm(comps)
    w_in_b = jnp.take(w_in, perm, axis=2).astype(BF16)
    q_gain = a_q_norm[:, _head_perm(1)]
    k_gain = a_k_norm[:, _head_perm(1)]
    w_gate_b = jnp.transpose(w_gate_up.reshape(depth, rank, 3, d), (0, 2, 1, 3)).astype(BF16)
    b_gate_r = b_gate.reshape(depth, 3, 1, d)
    w_branch_b = w_branch.astype(BF16)
    w_out_b = w_out.astype(BF16)
    w_ffn_in_b = w_ffn_in.astype(BF16)
    w_ffn_out_b = w_ffn_out.astype(BF16)

    tabs = _rope_tables(seq, n_ctx, batch, HEAD_DIM, 1) + _rope_tables(seq, n_ctx, batch, HEAD_DIM // 2, 2)
    plan = _nbr_plan(seq)
    nbr_bias = _nbr_bias(b_rel_bias, plan[1])

    rows = -(-(batch + 1) // 8) * 8
    cond = jnp.zeros((rows, d), F32).at[:batch].set(c).at[batch].set(c_ctx)
    mods = _adaln(cond, ada_down, ada_up, ada_bias).reshape(depth, 6, rows, 1, d)

    xs = jnp.concatenate([ctx, x], axis=1).reshape(m, d)
    h = _prenorm(xs, norm_mix_pre[0], mods, 0, 0, 1, geom)
    for l in range(depth):
        lam_init = 0.8 - 0.6 * math.exp(-0.3 * l)
        proj = _inproj(h, w_in_b[l], q_gain[l], k_gain[l], tabs, cfg)
        proj3 = proj.reshape(batch, t, in_w)
        o_a = _attn_a(proj3, cfg).reshape(m, -1)
        o_b = _attn_b(proj3, nbr_bias, l, plan, cfg).reshape(m, -1)
        o_c = _attn_c(proj3, (c_lambda_q1[l], c_lambda_k1[l], c_lambda_q2[l], c_lambda_k2[l]),
                      c_subln[l], lam_init, cfg).reshape(m, -1)
        merged = _merge(proj, o_a, o_b, o_c, w_gate_b[l], b_gate_r[l], w_branch_b[l], cfg)
        mix = _matmul_f32(merged, w_out_b[l], cfg["mm_tm"], cfg["mm_tn"], "out_proj")
        xs, h = _resid(xs, mix, norm_mix_post[l], norm_ffn_pre[l], mods, l, 2, l, 3, 4, geom)
        act = _ffn_in(h, w_ffn_in_b[l], cfg)
        f = _matmul_f32(act, w_ffn_out_b[l], min(512, cfg["mm_tm"]), cfg["mm_tn"], "ffn_out")
        nxt = min(l + 1, depth - 1)
        xs, h = _resid(xs, f, norm_ffn_post[l], norm_mix_pre[nxt], mods, l, 5, nxt, 0, 1, geom)
    return xs.reshape(batch, t, d)[:, n_ctx:]
```

```python
import functools
import math

import jax
import jax.numpy as jnp
import numpy as np
from jax import lax
from jax.experimental import pallas as pl
from jax.experimental.pallas import tpu as pltpu

HEAD_DIM = 128
GRID_W = 64
NA_ROWS = 8
NA_COLS = 16
Q_GROUP = 4
ROPE_THETA = 10000.0
EPS = 1e-6
LOG2_E = math.log2(math.e)
ROW_TILE = 256
MM_TILE_M = 1024
VMEM_LIMIT = 56 * 1024 * 1024
BF16 = jnp.bfloat16
F32 = jnp.float32


def _params(*sem):
    return pltpu.CompilerParams(dimension_semantics=sem, vmem_limit_bytes=VMEM_LIMIT)


def _dot(a, b):
    return jnp.dot(a, b, preferred_element_type=F32)


def _dot_nt(a, b):
    return lax.dot_general(a, b, (((1,), (1,)), ((), ())), preferred_element_type=F32)


def _rms(x, gain):
    return x * lax.rsqrt(jnp.mean(x * x, axis=-1, keepdims=True) + EPS) * gain


def _adaln_kernel(cond_ref, down_ref, up_ref, bias_ref, o_ref):
    cond = cond_ref[...]
    act = (cond * jax.nn.sigmoid(cond)).astype(BF16)
    low = _dot(act, down_ref[0].astype(BF16))
    o_ref[0, 0] = _dot(low.astype(BF16), up_ref[0].astype(BF16)) + bias_ref[0]


def _adaln(cond, ada_down, ada_up, ada_bias):
    depth, d, rank = ada_down.shape
    rows = cond.shape[0]
    return pl.pallas_call(
        _adaln_kernel,
        grid=(depth, 6),
        in_specs=[
            pl.BlockSpec((rows, d), lambda l, k: (0, 0)),
            pl.BlockSpec((1, d, rank), lambda l, k: (l, 0, 0)),
            pl.BlockSpec((1, rank, d), lambda l, k: (l, 0, k)),
            pl.BlockSpec((1, 1, d), lambda l, k: (l, 0, k)),
        ],
        out_specs=pl.BlockSpec((1, 1, rows, d), lambda l, k: (l, k, 0, 0)),
        out_shape=jax.ShapeDtypeStruct((depth, 6, rows, d), F32),
        compiler_params=_params("arbitrary", "arbitrary"),
        name="adaln",
    )(cond, ada_down, ada_up, ada_bias.reshape(depth, 1, 6 * d))


def _mod_row(i, tiles_per_batch, ctx_tiles, batch):
    return jnp.where(i % tiles_per_batch < ctx_tiles, batch, i // tiles_per_batch)


def _prenorm_kernel(x_ref, gain_ref, shift_ref, scale_ref, h_ref):
    y = _rms(x_ref[...], gain_ref[...])
    h_ref[...] = (y * (1.0 + scale_ref[0, 0, 0]) + shift_ref[0, 0, 0]).astype(BF16)


def _prenorm(xs, gain, mods, layer, k_shift, k_scale, geom):
    m, d = xs.shape
    tpb, ctx_tiles, batch = geom
    row = functools.partial(_mod_row, tiles_per_batch=tpb, ctx_tiles=ctx_tiles, batch=batch)
    return pl.pallas_call(
        _prenorm_kernel,
        grid=(m // ROW_TILE,),
        in_specs=[
            pl.BlockSpec((ROW_TILE, d), lambda i: (i, 0)),
            pl.BlockSpec((1, d), lambda i: (0, 0)),
            pl.BlockSpec((1, 1, 1, 1, d), lambda i: (layer, k_shift, row(i), 0, 0)),
            pl.BlockSpec((1, 1, 1, 1, d), lambda i: (layer, k_scale, row(i), 0, 0)),
        ],
        out_specs=pl.BlockSpec((ROW_TILE, d), lambda i: (i, 0)),
        out_shape=jax.ShapeDtypeStruct((m, d), BF16),
        compiler_params=_params("arbitrary"),
        name="prenorm",
    )(xs, gain.reshape(1, d), mods, mods)


def _resid_kernel(x_ref, y_ref, post_ref, gate_ref, pre_ref, shift_ref, scale_ref, xo_ref, h_ref):
    x_new = x_ref[...] + gate_ref[0, 0, 0] * _rms(y_ref[...], post_ref[...])
    xo_ref[...] = x_new
    h = _rms(x_new, pre_ref[...])
    h_ref[...] = (h * (1.0 + scale_ref[0, 0, 0]) + shift_ref[0, 0, 0]).astype(BF16)


def _resid(xs, y, post_gain, pre_gain, mods, layer, k_gate, mod_layer, k_shift, k_scale, geom):
    m, d = xs.shape
    tpb, ctx_tiles, batch = geom
    row = functools.partial(_mod_row, tiles_per_batch=tpb, ctx_tiles=ctx_tiles, batch=batch)
    tile = pl.BlockSpec((ROW_TILE, d), lambda i: (i, 0))
    vec = pl.BlockSpec((1, d), lambda i: (0, 0))
    return pl.pallas_call(
        _resid_kernel,
        grid=(m // ROW_TILE,),
        in_specs=[
            tile, tile, vec,
            pl.BlockSpec((1, 1, 1, 1, d), lambda i: (layer, k_gate, row(i), 0, 0)),
            vec,
            pl.BlockSpec((1, 1, 1, 1, d), lambda i: (mod_layer, k_shift, row(i), 0, 0)),
            pl.BlockSpec((1, 1, 1, 1, d), lambda i: (mod_layer, k_scale, row(i), 0, 0)),
        ],
        out_specs=[tile, tile],
        out_shape=[jax.ShapeDtypeStruct((m, d), F32), jax.ShapeDtypeStruct((m, d), BF16)],
        compiler_params=_params("arbitrary"),
        name="resid",
    )(xs, y, post_gain.reshape(1, d), mods, pre_gain.reshape(1, d), mods, mods)


def _rope(y, cos, sin):
    return y * cos + pltpu.roll(y, HEAD_DIM // 2, 1) * sin


def _inproj_kernel(h_ref, w_ref, qg_ref, kg_ref, cosa_ref, sina_ref, cosc_ref, sinc_ref, o_ref,
                   *, tn, q_tiles, k_tiles, rope_c_ranges):
    j = pl.program_id(1)
    acc = _dot(h_ref[...], w_ref[...])
    heads = tn // HEAD_DIM

    def store(fn):
        for hh in range(heads):
            sl = slice(hh * HEAD_DIM, (hh + 1) * HEAD_DIM)
            o_ref[:, sl] = fn(acc[:, sl]).astype(BF16)

    is_q = j < q_tiles
    is_k = jnp.logical_and(j >= q_tiles, j < q_tiles + k_tiles)
    is_c = functools.reduce(jnp.logical_or,
                            [jnp.logical_and(j >= lo, j < hi) for lo, hi in rope_c_ranges])
    plain = jnp.logical_not(jnp.logical_or(jnp.logical_or(is_q, is_k), is_c))

    @pl.when(is_q)
    def _():
        store(lambda a: _rope(_rms(a, qg_ref[...]), cosa_ref[...], sina_ref[...]))

    @pl.when(is_k)
    def _():
        store(lambda a: _rope(_rms(a, kg_ref[...]), cosa_ref[...], sina_ref[...]))

    @pl.when(is_c)
    def _():
        store(lambda a: _rope(a, cosc_ref[...], sinc_ref[...]))

    @pl.when(plain)
    def _():
        o_ref[...] = acc.astype(BF16)


def _inproj(h, w, layer, q_gain, k_gain, tabs, cfg):
    m, d = h.shape
    n = w.shape[2]
    tn, tm = cfg["in_tn"], cfg["mm_tm"]
    tab = pl.BlockSpec((tm, HEAD_DIM), lambda i, j: (i, 0))
    vec = pl.BlockSpec((1, HEAD_DIM), lambda i, j: (0, 0))
    kern = functools.partial(
        _inproj_kernel, tn=tn, q_tiles=cfg["a_q_w"] // tn, k_tiles=cfg["a_kv_w"] // tn,
        rope_c_ranges=[(cfg["off_cq"] // tn, (cfg["off_cq"] + 2 * cfg["c_qk_w"]) // tn)])
    return pl.pallas_call(
        kern,
        grid=(m // tm, n // tn),
        in_specs=[
            pl.BlockSpec((tm, d), lambda i, j: (i, 0)),
            pl.BlockSpec((None, d, tn), lambda i, j: (layer, 0, j)),
            vec, vec, tab, tab, tab, tab,
        ],
        out_specs=pl.BlockSpec((tm, tn), lambda i, j: (i, j)),
        out_shape=jax.ShapeDtypeStruct((m, n), BF16),
        compiler_params=_params("arbitrary", "arbitrary"),
        name="inproj",
    )(h, w, q_gain.reshape(1, HEAD_DIM), k_gain.reshape(1, HEAD_DIM), *tabs)


def _softmax_pv(s, v):
    m = jnp.max(s, axis=-1, keepdims=True)
    p = jnp.exp(s - m)
    l = jnp.sum(p, axis=-1, keepdims=True)
    return _dot(p.astype(BF16), v) / l


ONES_ROWS = 16
EXP_CHUNK = 256


def _attend_t(q, k, vt, c):
    st = _dot_nt(k, q)
    m = jnp.max(st, axis=0, keepdims=True)
    acc = _dot(vt, jnp.exp2((st - m) * c).astype(BF16))
    d = vt.shape[0] - ONES_ROWS
    return acc[:d] / acc[d:d + 1]


def _qk_max(q, k, st_ref):
    st = _dot_nt(k, q)
    st_ref[...] = st
    return jnp.max(st, axis=0, keepdims=True)


def _exp_pv(st_ref, pt_ref, vt, m, c):
    n = st_ref.shape[0]
    for c0 in range(0, n, EXP_CHUNK):
        c1 = min(c0 + EXP_CHUNK, n)
        pt_ref[c0:c1] = jnp.exp2((st_ref[c0:c1] - m) * c).astype(BF16)
    acc = _dot(vt, pt_ref[...])
    d = vt.shape[0] - ONES_ROWS
    return acc[:d] / acc[d:d + 1]


def _attend_tiles(n_iter, n_items, load_q, store_o, k, vt, st_ref, pt_ref, m_ref, c):
    def scores(i, j):
        m_ref[j] = _qk_max(load_q(i, j), k, st_ref.at[j])

    for j in range(n_items - 1):
        scores(0, j)

    def body(i, carry):
        nxt = jnp.minimum(i + 1, n_iter - 1)
        scores(i, n_items - 1)
        for j in range(n_items):
            store_o(i, j, _exp_pv(st_ref.at[j], pt_ref.at[j % 2], vt, m_ref[j], c))
            if j < n_items - 1:
                scores(nxt, j)
        return carry

    lax.fori_loop(0, n_iter, body, 0)


def _fill_vt(vt_ref, v_ref):
    vt_ref[:HEAD_DIM] = v_ref[0].T
    vt_ref[HEAD_DIM:] = jnp.ones((ONES_ROWS, vt_ref.shape[1]), BF16)


def _attn_scratch(t, n_items):
    return [pltpu.VMEM((HEAD_DIM + ONES_ROWS, t), BF16), pltpu.VMEM((n_items, t, ROW_TILE), F32),
            pltpu.VMEM((2, t, ROW_TILE), BF16), pltpu.VMEM((n_items, 1, ROW_TILE), F32)]


def _attn_a_kernel(q_ref, k_ref, v_ref, o_ref, vt_ref, st_ref, pt_ref, m_ref, *, n_ctx, group, c):
    t = k_ref.shape[1]
    tq = ROW_TILE
    _fill_vt(vt_ref, v_ref)
    heads = [slice(g * HEAD_DIM, (g + 1) * HEAD_DIM) for g in range(group)]

    for r0 in range(0, n_ctx, tq):
        for sl in heads:
            o = _attend_t(q_ref[0, r0:r0 + tq, sl], k_ref[0, :n_ctx], vt_ref[:, :n_ctx], c)
            o_ref[0, r0:r0 + tq, sl] = o.T.astype(BF16)

    def rows(i):
        return pl.ds(pl.multiple_of(n_ctx + i * tq, tq), tq)

    def load_q(i, g):
        return q_ref[0, rows(i), heads[g]]

    def store_o(i, g, o):
        o_ref[0, rows(i), heads[g]] = o.T.astype(BF16)

    _attend_tiles((t - n_ctx) // tq, group, load_q, store_o, k_ref[0], vt_ref[...], st_ref, pt_ref, m_ref, c)


def _attn_a(proj, cfg):
    b, t, _ = proj.shape
    gw = Q_GROUP * HEAD_DIM
    kvh = cfg["a_kv_w"] // HEAD_DIM
    k_blk = cfg["a_q_w"] // HEAD_DIM
    v_blk = (cfg["a_q_w"] + cfg["a_kv_w"]) // HEAD_DIM
    kern = functools.partial(_attn_a_kernel, n_ctx=cfg["n_ctx"], group=Q_GROUP, c=HEAD_DIM ** -0.5 * LOG2_E)
    return pl.pallas_call(
        kern,
        grid=(b, kvh),
        in_specs=[
            pl.BlockSpec((1, t, gw), lambda bi, hi: (bi, 0, hi)),
            pl.BlockSpec((1, t, HEAD_DIM), lambda bi, hi: (bi, 0, k_blk + hi)),
            pl.BlockSpec((1, t, HEAD_DIM), lambda bi, hi: (bi, 0, v_blk + hi)),
        ],
        out_specs=pl.BlockSpec((1, t, gw), lambda bi, hi: (bi, 0, hi)),
        out_shape=jax.ShapeDtypeStruct((b, t, cfg["a_q_w"]), BF16),
        scratch_shapes=_attn_scratch(t, Q_GROUP),
        compiler_params=_params("arbitrary", "arbitrary"),
        name="attn_a",
    )(proj, proj, proj)


def _attn_c_kernel(q_ref, k_ref, v_ref, l1_ref, l2_ref, l3_ref, l4_ref, gain_ref, o_ref,
                   vt_ref, st_ref, pt_ref, m_ref, *, n_ctx, c, lam_init):
    t = k_ref.shape[1]
    tq = ROW_TILE
    _fill_vt(vt_ref, v_ref)
    lam = (jnp.exp(jnp.sum(l1_ref[...] * l2_ref[...], axis=-1, keepdims=True))
           - jnp.exp(jnp.sum(l3_ref[...] * l4_ref[...], axis=-1, keepdims=True)) + lam_init)
    lane = lax.broadcasted_iota(jnp.int32, (tq, HEAD_DIM), 1)
    first = (lane % (HEAD_DIM // 2)) < (HEAD_DIM // 4)

    def component(q, comp):
        return jnp.where(first == (comp == 0), q, jnp.zeros_like(q))

    def finish(o1, o2):
        o = (o1 - lam * o2).T
        return (_rms(o, gain_ref[...]) * (1.0 - lam_init)).astype(BF16)

    for r0 in range(0, n_ctx, tq):
        q = q_ref[0, r0:r0 + tq]
        o1, o2 = [_attend_t(component(q, comp), k_ref[0, :n_ctx], vt_ref[:, :n_ctx], c) for comp in range(2)]
        o_ref[0, r0:r0 + tq] = finish(o1, o2)

    def rows(i, j):
        return pl.ds(pl.multiple_of(n_ctx + (2 * i + j // 2) * tq, tq), tq)

    def load_q(i, j):
        return component(q_ref[0, rows(i, j)], j % 2)

    pending = {}

    def store_o(i, j, o):
        if j % 2 == 0:
            pending[0] = o
        else:
            o_ref[0, rows(i, j)] = finish(pending.pop(0), o)

    _attend_tiles((t - n_ctx) // (2 * tq), 4, load_q, store_o, k_ref[0], vt_ref[...], st_ref, pt_ref, m_ref, c)


def _attn_c(proj, lams, subln, lam_init, cfg):
    b, t, _ = proj.shape
    heads = cfg["c_v_w"] // HEAD_DIM
    q_blk = cfg["off_cq"] // HEAD_DIM
    k_blk = q_blk + cfg["c_qk_w"] // HEAD_DIM
    v_blk = k_blk + cfg["c_qk_w"] // HEAD_DIM
    qk_dim = HEAD_DIM // 2
    lam_spec = pl.BlockSpec((1, qk_dim), lambda bi, hi: (0, 0))
    kern = functools.partial(_attn_c_kernel, n_ctx=cfg["n_ctx"], c=qk_dim ** -0.5 * LOG2_E, lam_init=lam_init)
    return pl.pallas_call(
        kern,
        grid=(b, heads),
        in_specs=[
            pl.BlockSpec((1, t, HEAD_DIM), lambda bi, hi: (bi, 0, q_blk + hi)),
            pl.BlockSpec((1, t, HEAD_DIM), lambda bi, hi: (bi, 0, k_blk + hi)),
            pl.BlockSpec((1, t, HEAD_DIM), lambda bi, hi: (bi, 0, v_blk + hi)),
            lam_spec, lam_spec, lam_spec, lam_spec,
            pl.BlockSpec((1, HEAD_DIM), lambda bi, hi: (0, 0)),
        ],
        out_specs=pl.BlockSpec((1, t, HEAD_DIM), lambda bi, hi: (bi, 0, hi)),
        out_shape=jax.ShapeDtypeStruct((b, t, cfg["c_v_w"]), BF16),
        scratch_shapes=_attn_scratch(t, 4),
        compiler_params=_params("arbitrary", "arbitrary"),
        name="attn_c",
    )(proj, proj, proj, *[v.reshape(1, qk_dim) for v in lams], subln.reshape(1, HEAD_DIM))


def _attn_b_kernel(q_ref, k_ref, v_ref, bias_ref, o_ref, *, n_ctx, blocks, q_rows, k_rows, scale):
    k_ctx = k_ref[0, :n_ctx]
    v_ctx = v_ref[0, :n_ctx]
    o_ref[0, :n_ctx] = _softmax_pv(_dot_nt(q_ref[0, :n_ctx], k_ctx) * scale, v_ctx).astype(BF16)
    for q0, k0, case in blocks:
        q = q_ref[0, n_ctx + q0:n_ctx + q0 + q_rows]
        k_win = k_ref[0, n_ctx + k0:n_ctx + k0 + k_rows]
        v_win = v_ref[0, n_ctx + k0:n_ctx + k0 + k_rows]
        s_loc = _dot_nt(q, k_win) * scale + bias_ref[0, case]
        s_ctx = _dot_nt(q, k_ctx) * scale
        m = jnp.maximum(jnp.max(s_loc, axis=-1, keepdims=True), jnp.max(s_ctx, axis=-1, keepdims=True))
        p_loc = jnp.exp(s_loc - m)
        p_ctx = jnp.exp(s_ctx - m)
        l = jnp.sum(p_loc, axis=-1, keepdims=True) + jnp.sum(p_ctx, axis=-1, keepdims=True)
        o = (_dot(p_loc.astype(BF16), v_win) + _dot(p_ctx.astype(BF16), v_ctx)) / l
        o_ref[0, n_ctx + q0:n_ctx + q0 + q_rows] = o.astype(BF16)


def _nbr_plan(seq):
    rows = seq // GRID_W
    kr = min(NA_ROWS, rows)
    q_rows_n = min(NA_ROWS, rows)
    win = min(2 * NA_ROWS, rows)
    cases, blocks = [], []
    for i in range(rows // q_rows_n):
        ws = int(np.clip(i * q_rows_n - kr // 2, 0, rows - win))
        drmap = np.full((q_rows_n, win), -1, np.int64)
        for a in range(q_rows_n):
            r = i * q_rows_n + a
            r0 = int(np.clip(r - kr // 2, 0, rows - kr))
            for w in range(win):
                kr_abs = ws + w
                if r0 <= kr_abs < r0 + kr:
                    drmap[a, w] = kr_abs - r + (NA_ROWS - 1)
        key = drmap.tobytes()
        keys = [c.tobytes() for c in cases]
        if key not in keys:
            cases.append(drmap)
            keys.append(key)
        blocks.append((i * q_rows_n * GRID_W, ws * GRID_W, keys.index(key)))
    return blocks, np.stack(cases), q_rows_n * GRID_W, win * GRID_W


def _nbr_bias(rpb, case_maps):
    depth, heads = rpb.shape[:2]
    c = np.arange(GRID_W)
    c0 = np.clip(c - NA_COLS // 2, 0, GRID_W - NA_COLS)
    col_mask = (c[None, :] >= c0[:, None]) & (c[None, :] < c0[:, None] + NA_COLS)
    col_idx = np.clip(c[None, :] - c[:, None] + NA_COLS - 1, 0, 2 * NA_COLS - 2)
    toe = jnp.where(col_mask, rpb.astype(F32)[:, :, :, col_idx], -jnp.inf)
    toe = jnp.concatenate([toe, jnp.full((depth, heads, 1, GRID_W, GRID_W), -jnp.inf, F32)], axis=2)
    n_case, qr, win = case_maps.shape
    idx = np.where(case_maps < 0, 2 * NA_ROWS - 1, case_maps).reshape(-1)
    blk = jnp.take(toe, idx, axis=2).reshape(depth, heads, n_case, qr, win, GRID_W, GRID_W)
    return jnp.transpose(blk, (0, 1, 2, 3, 5, 4, 6)).reshape(depth, heads, n_case, qr * GRID_W, win * GRID_W)


def _attn_b(proj, bias, layer, plan, cfg):
    b, t, _ = proj.shape
    heads = cfg["b_w"] // HEAD_DIM
    q_blk = cfg["off_bq"] // HEAD_DIM
    k_blk = q_blk + heads
    v_blk = k_blk + heads
    blocks, _, q_rows, k_rows = plan
    n_case = bias.shape[2]
    kern = functools.partial(_attn_b_kernel, n_ctx=cfg["n_ctx"], blocks=blocks, q_rows=q_rows,
                             k_rows=k_rows, scale=HEAD_DIM ** -0.5)
    return pl.pallas_call(
        kern,
        grid=(heads, b),
        in_specs=[
            pl.BlockSpec((1, t, HEAD_DIM), lambda hi, bi: (bi, 0, q_blk + hi)),
            pl.BlockSpec((1, t, HEAD_DIM), lambda hi, bi: (bi, 0, k_blk + hi)),
            pl.BlockSpec((1, t, HEAD_DIM), lambda hi, bi: (bi, 0, v_blk + hi)),
            pl.BlockSpec((None, 1, n_case, q_rows, k_rows), lambda hi, bi: (layer, hi, 0, 0, 0)),
        ],
        out_specs=pl.BlockSpec((1, t, HEAD_DIM), lambda hi, bi: (bi, 0, hi)),
        out_shape=jax.ShapeDtypeStruct((b, t, cfg["b_w"]), BF16),
        compiler_params=_params("arbitrary", "arbitrary"),
        name="attn_b",
    )(proj, proj, proj, bias)


def _merge_kernel(g_ref, oa_ref, ob_ref, oc_ref, wg_ref, bg_ref, wa_ref, wb_ref, wc_ref, o_ref):
    g_low = g_ref[...]
    acc = None
    for br, (o_br, w_br) in enumerate(((oa_ref, wa_ref), (ob_ref, wb_ref), (oc_ref, wc_ref))):
        gate = jax.nn.sigmoid(_dot(g_low, wg_ref[br]) + bg_ref[br])
        term = gate * _dot(o_br[...], w_br[...])
        acc = term if acc is None else acc + term
    o_ref[...] = acc.astype(BF16)


def _merge(proj2, o_a, o_b, o_c, w_gate, b_gate, w_branch, layer, cfg):
    m = proj2.shape[0]
    d = w_branch.shape[2]
    rank = w_gate.shape[2]
    tm, tn = cfg["mm_tm"], cfg["mm_tn"]
    a_w, b_w, c_w = cfg["a_q_w"], cfg["b_w"], cfg["c_v_w"]
    g_blk = cfg["off_g"] // rank
    return pl.pallas_call(
        _merge_kernel,
        grid=(m // tm, d // tn),
        in_specs=[
            pl.BlockSpec((tm, rank), lambda i, j: (i, g_blk)),
            pl.BlockSpec((tm, a_w), lambda i, j: (i, 0)),
            pl.BlockSpec((tm, b_w), lambda i, j: (i, 0)),
            pl.BlockSpec((tm, c_w), lambda i, j: (i, 0)),
            pl.BlockSpec((None, 3, rank, tn), lambda i, j: (layer, 0, 0, j)),
            pl.BlockSpec((None, 3, 1, tn), lambda i, j: (layer, 0, 0, j)),
            pl.BlockSpec((None, a_w, tn), lambda i, j: (layer, 0, j)),
            pl.BlockSpec((None, b_w, tn), lambda i, j: (layer, a_w // b_w, j)),
            pl.BlockSpec((None, c_w, tn), lambda i, j: (layer, (a_w + b_w) // c_w, j)),
        ],
        out_specs=pl.BlockSpec((tm, tn), lambda i, j: (i, j)),
        out_shape=jax.ShapeDtypeStruct((m, d), BF16),
        compiler_params=_params("arbitrary", "arbitrary"),
        name="merge",
    )(proj2, o_a, o_b, o_c, w_gate, b_gate, w_branch, w_branch, w_branch)


def _matmul_kernel(a_ref, w_ref, o_ref):
    o_ref[...] = _dot(a_ref[...], w_ref[...])


def _matmul_f32(a, w, layer, tm, tn, name):
    m, k = a.shape
    n = w.shape[2]
    return pl.pallas_call(
        _matmul_kernel,
        grid=(m // tm, n // tn),
        in_specs=[pl.BlockSpec((tm, k), lambda i, j: (i, 0)),
                  pl.BlockSpec((None, k, tn), lambda i, j: (layer, 0, j))],
        out_specs=pl.BlockSpec((tm, tn), lambda i, j: (i, j)),
        out_shape=jax.ShapeDtypeStruct((m, n), F32),
        compiler_params=_params("arbitrary", "arbitrary"),
        name=name,
    )(a, w)


def _ffn_in_kernel(h_ref, wg_ref, wu_ref, o_ref):
    h = h_ref[...]
    gate = _dot(h, wg_ref[...])
    up = _dot(h, wu_ref[...])
    o_ref[...] = (gate * jax.nn.sigmoid(gate) * up).astype(BF16)


def _ffn_in(h, w, layer, cfg):
    m, d = h.shape
    d_ff = w.shape[2] // 2
    tm, tn = cfg["mm_tm"], cfg["ffn_tn"]
    up_blk = d_ff // tn
    return pl.pallas_call(
        _ffn_in_kernel,
        grid=(m // tm, d_ff // tn),
        in_specs=[pl.BlockSpec((tm, d), lambda i, j: (i, 0)),
                  pl.BlockSpec((None, d, tn), lambda i, j: (layer, 0, j)),
                  pl.BlockSpec((None, d, tn), lambda i, j: (layer, 0, up_blk + j))],
        out_specs=pl.BlockSpec((tm, tn), lambda i, j: (i, j)),
        out_shape=jax.ShapeDtypeStruct((m, d_ff), BF16),
        compiler_params=_params("arbitrary", "arbitrary"),
        name="ffn_in",
    )(h, w, w)


def _rope_tables(seq, n_ctx, batch, dim, comps):
    t = np.arange(seq)
    rows = (t // GRID_W).astype(np.float32)
    cols = (t % GRID_W).astype(np.float32)
    n_pairs = dim // 4
    freqs = jnp.asarray(ROPE_THETA, F32) ** (-jnp.arange(n_pairs, dtype=F32) / n_pairs)
    ang = jnp.concatenate([jnp.asarray(rows)[:, None] * freqs, jnp.asarray(cols)[:, None] * freqs], axis=-1)
    cos = jnp.tile(jnp.cos(ang), (1, 2 * comps))
    sin = jnp.tile(jnp.sin(ang), (1, comps))
    sin = jnp.concatenate([-sin, sin], axis=-1)
    cos = jnp.concatenate([jnp.ones((n_ctx, HEAD_DIM), F32), cos], axis=0)
    sin = jnp.concatenate([jnp.zeros((n_ctx, HEAD_DIM), F32), sin], axis=0)
    return jnp.tile(cos, (batch, 1)), jnp.tile(sin, (batch, 1))


def _head_perm(comps):
    pairs = HEAD_DIM // (2 * comps)
    perm = np.zeros(HEAD_DIM, np.int64)
    for e in range(2):
        for c in range(comps):
            for p in range(pairs):
                perm[e * (HEAD_DIM // 2) + c * pairs + p] = c * (HEAD_DIM // comps) + 2 * p + e
    return perm


def _config(d, n_ctx, in_w, rank, d_ff):
    heads = d // HEAD_DIM
    a_q_w = (heads // 2) * HEAD_DIM
    a_kv_w = a_q_w // Q_GROUP
    b_w = (heads // 4) * HEAD_DIM
    c_qk_w = c_v_w = (heads // 4) * HEAD_DIM
    off_bq = a_q_w + 2 * a_kv_w
    off_cq = off_bq + 3 * b_w
    off_g = off_cq + 2 * c_qk_w + c_v_w
    assert off_g + rank == in_w and off_g % rank == 0
    assert a_q_w % b_w == 0 and (a_q_w + b_w) % c_v_w == 0
    in_tn = math.gcd(math.gcd(a_kv_w, b_w), 512)
    ffn_tn = math.gcd(d_ff, 512)
    return dict(n_ctx=n_ctx, a_q_w=a_q_w, a_kv_w=a_kv_w, b_w=b_w, c_qk_w=c_qk_w, c_v_w=c_v_w,
                off_bq=off_bq, off_cq=off_cq, off_g=off_g, in_tn=in_tn, ffn_tn=ffn_tn,
                mm_tn=min(512, d))


def kernel(x, c, ctx, c_ctx, ada_down, ada_up, ada_bias, norm_mix_pre, norm_mix_post, norm_ffn_pre,
           norm_ffn_post, w_in, a_q_norm, a_k_norm, b_rel_bias, c_lambda_q1, c_lambda_k1, c_lambda_q2,
           c_lambda_k2, c_subln, w_gate_up, b_gate, w_branch, w_out, w_ffn_in, w_ffn_out):
    batch, seq, d = x.shape
    n_ctx = ctx.shape[1]
    depth, _, in_w = w_in.shape
    rank = w_gate_up.shape[1]
    d_ff = w_ffn_out.shape[1]
    t = n_ctx + seq
    m = batch * t
    cfg = _config(d, n_ctx, in_w, rank, d_ff)
    assert n_ctx % ROW_TILE == 0 and seq % ROW_TILE == 0 and seq % (NA_ROWS * GRID_W) == 0
    cfg["mm_tm"] = math.gcd(m, MM_TILE_M)
    geom = (t // ROW_TILE, n_ctx // ROW_TILE, batch)

    perm = np.arange(in_w)
    for off, width, comps in ((0, cfg["a_q_w"] + cfg["a_kv_w"], 1), (cfg["off_cq"], 2 * cfg["c_qk_w"], 2)):
        for hstart in range(off, off + width, HEAD_DIM):
            perm[hstart:hstart + HEAD_DIM] = hstart + _head_perm(comps)
    w_in_b = jnp.take(w_in, perm, axis=2).astype(BF16)
    q_gain = a_q_norm[:, _head_perm(1)]
    k_gain = a_k_norm[:, _head_perm(1)]
    w_gate_b = jnp.transpose(w_gate_up.reshape(depth, rank, 3, d), (0, 2, 1, 3)).astype(BF16)
    b_gate_r = b_gate.reshape(depth, 3, 1, d)
    w_branch_b = w_branch.astype(BF16)
    w_out_b = w_out.astype(BF16)
    w_ffn_in_b = w_ffn_in.astype(BF16)
    w_ffn_out_b = w_ffn_out.astype(BF16)

    tabs = _rope_tables(seq, n_ctx, batch, HEAD_DIM, 1) + _rope_tables(seq, n_ctx, batch, HEAD_DIM // 2, 2)
    plan = _nbr_plan(seq)
    nbr_bias = _nbr_bias(b_rel_bias, plan[1])

    rows = -(-(batch + 1) // 8) * 8
    cond = jnp.zeros((rows, d), F32).at[:batch].set(c).at[batch].set(c_ctx)
    mods = _adaln(cond, ada_down, ada_up, ada_bias).reshape(depth, 6, rows, 1, d)

    xs = jnp.concatenate([ctx, x], axis=1).reshape(m, d)
    h = _prenorm(xs, norm_mix_pre[0], mods, 0, 0, 1, geom)
    for l in range(depth):
        lam_init = 0.8 - 0.6 * math.exp(-0.3 * l)
        proj = _inproj(h, w_in_b, l, q_gain[l], k_gain[l], tabs, cfg)
        proj3 = proj.reshape(batch, t, in_w)
        o_a = _attn_a(proj3, cfg).reshape(m, -1)
        o_b = _attn_b(proj3, nbr_bias, l, plan, cfg).reshape(m, -1)
        o_c = _attn_c(proj3, (c_lambda_q1[l], c_lambda_k1[l], c_lambda_q2[l], c_lambda_k2[l]),
                      c_subln[l], lam_init, cfg).reshape(m, -1)
        merged = _merge(proj, o_a, o_b, o_c, w_gate_b, b_gate_r, w_branch_b, l, cfg)
        mix = _matmul_f32(merged, w_out_b, l, cfg["mm_tm"], cfg["mm_tn"], "out_proj")
        xs, h = _resid(xs, mix, norm_mix_post[l], norm_ffn_pre[l], mods, l, 2, l, 3, 4, geom)
        act = _ffn_in(h, w_ffn_in_b, l, cfg)
        f = _matmul_f32(act, w_ffn_out_b, l, min(512, cfg["mm_tm"]), cfg["mm_tn"], "ffn_out")
        nxt = min(l + 1, depth - 1)
        xs, h = _resid(xs, f, norm_ffn_post[l], norm_mix_pre[nxt], mods, l, 5, nxt, 0, 1, geom)
    return xs.reshape(batch, t, d)[:, n_ctx:]
```

```python
import functools
import math

import jax
import jax.numpy as jnp
import numpy as np
from jax import lax
from jax.experimental import pallas as pl
from jax.experimental.pallas import tpu as pltpu

HEAD_DIM = 128
GRID_W = 64
NA_ROWS = 8
NA_COLS = 16
Q_GROUP = 4
ROPE_THETA = 10000.0
EPS = 1e-6
LOG2_E = math.log2(math.e)
ROW_TILE = 256
MM_TILE_M = 1024
SUB_ROWS = 256
VMEM_LIMIT = 56 * 1024 * 1024
BF16 = jnp.bfloat16
F32 = jnp.float32


def _params(*sem):
    return pltpu.CompilerParams(dimension_semantics=sem, vmem_limit_bytes=VMEM_LIMIT)


def _dot(a, b):
    return jnp.dot(a, b, preferred_element_type=F32)


def _dot_nt(a, b):
    return lax.dot_general(a, b, (((1,), (1,)), ((), ())), preferred_element_type=F32)


def _rms(x, gain):
    return x * lax.rsqrt(jnp.mean(x * x, axis=-1, keepdims=True) + EPS) * gain


def _adaln_kernel(cond_ref, down_ref, up_ref, bias_ref, o_ref):
    cond = cond_ref[...]
    act = (cond * jax.nn.sigmoid(cond)).astype(BF16)
    low = _dot(act, down_ref[0].astype(BF16))
    o_ref[0, 0] = _dot(low.astype(BF16), up_ref[0].astype(BF16)) + bias_ref[0]


def _adaln(cond, ada_down, ada_up, ada_bias):
    depth, d, rank = ada_down.shape
    rows = cond.shape[0]
    return pl.pallas_call(
        _adaln_kernel,
        grid=(depth, 6),
        in_specs=[
            pl.BlockSpec((rows, d), lambda l, k: (0, 0)),
            pl.BlockSpec((1, d, rank), lambda l, k: (l, 0, 0)),
            pl.BlockSpec((1, rank, d), lambda l, k: (l, 0, k)),
            pl.BlockSpec((1, 1, d), lambda l, k: (l, 0, k)),
        ],
        out_specs=pl.BlockSpec((1, 1, rows, d), lambda l, k: (l, k, 0, 0)),
        out_shape=jax.ShapeDtypeStruct((depth, 6, rows, d), F32),
        compiler_params=_params("arbitrary", "arbitrary"),
        name="adaln",
    )(cond, ada_down, ada_up, ada_bias.reshape(depth, 1, 6 * d))


def _mod_row(i, tiles_per_batch, ctx_tiles, batch):
    return jnp.where(i % tiles_per_batch < ctx_tiles, batch, i // tiles_per_batch)


def _prenorm_kernel(x_ref, gain_ref, shift_ref, scale_ref, h_ref):
    y = _rms(x_ref[...], gain_ref[...])
    h_ref[...] = (y * (1.0 + scale_ref[0, 0, 0]) + shift_ref[0, 0, 0]).astype(BF16)


def _prenorm(xs, gain, mods, layer, k_shift, k_scale, geom):
    m, d = xs.shape
    tpb, ctx_tiles, batch = geom
    row = functools.partial(_mod_row, tiles_per_batch=tpb, ctx_tiles=ctx_tiles, batch=batch)
    return pl.pallas_call(
        _prenorm_kernel,
        grid=(m // ROW_TILE,),
        in_specs=[
            pl.BlockSpec((ROW_TILE, d), lambda i: (i, 0)),
            pl.BlockSpec((1, d), lambda i: (0, 0)),
            pl.BlockSpec((1, 1, 1, 1, d), lambda i: (layer, k_shift, row(i), 0, 0)),
            pl.BlockSpec((1, 1, 1, 1, d), lambda i: (layer, k_scale, row(i), 0, 0)),
        ],
        out_specs=pl.BlockSpec((ROW_TILE, d), lambda i: (i, 0)),
        out_shape=jax.ShapeDtypeStruct((m, d), BF16),
        compiler_params=_params("arbitrary"),
        name="prenorm",
    )(xs, gain.reshape(1, d), mods, mods)


def _resid_kernel(x_ref, y_ref, post_ref, gate_ref, pre_ref, shift_ref, scale_ref, xo_ref, h_ref):
    x_new = x_ref[...] + gate_ref[0, 0, 0] * _rms(y_ref[...], post_ref[...])
    xo_ref[...] = x_new
    h = _rms(x_new, pre_ref[...])
    h_ref[...] = (h * (1.0 + scale_ref[0, 0, 0]) + shift_ref[0, 0, 0]).astype(BF16)


def _resid(xs, y, post_gain, pre_gain, mods, layer, k_gate, mod_layer, k_shift, k_scale, geom):
    m, d = xs.shape
    tpb, ctx_tiles, batch = geom
    row = functools.partial(_mod_row, tiles_per_batch=tpb, ctx_tiles=ctx_tiles, batch=batch)
    tile = pl.BlockSpec((ROW_TILE, d), lambda i: (i, 0))
    vec = pl.BlockSpec((1, d), lambda i: (0, 0))
    return pl.pallas_call(
        _resid_kernel,
        grid=(m // ROW_TILE,),
        in_specs=[
            tile, tile, vec,
            pl.BlockSpec((1, 1, 1, 1, d), lambda i: (layer, k_gate, row(i), 0, 0)),
            vec,
            pl.BlockSpec((1, 1, 1, 1, d), lambda i: (mod_layer, k_shift, row(i), 0, 0)),
            pl.BlockSpec((1, 1, 1, 1, d), lambda i: (mod_layer, k_scale, row(i), 0, 0)),
        ],
        out_specs=[tile, tile],
        out_shape=[jax.ShapeDtypeStruct((m, d), F32), jax.ShapeDtypeStruct((m, d), BF16)],
        compiler_params=_params("arbitrary"),
        name="resid",
    )(xs, y, post_gain.reshape(1, d), mods, pre_gain.reshape(1, d), mods, mods)


def _rope(y, cos, sin):
    lane = lax.broadcasted_iota(jnp.int32, y.shape, 1)
    partner = jnp.where(lane % 2 == 0, pltpu.roll(y, HEAD_DIM - 1, 1), pltpu.roll(y, 1, 1))
    return y * cos + partner * sin


def _inproj_kernel(h_ref, w_ref, qg_ref, kg_ref, cosa_ref, sina_ref, cosc_ref, sinc_ref, o_ref,
                   *, tn, q_tiles, k_tiles, rope_c_ranges):
    j = pl.program_id(1)
    heads = tn // HEAD_DIM
    n_sub = h_ref.shape[0] // SUB_ROWS

    def project(fn):
        for s in range(n_sub):
            rs = slice(s * SUB_ROWS, (s + 1) * SUB_ROWS)
            acc = _dot(h_ref[rs], w_ref[...].astype(BF16))
            if fn is None:
                o_ref[rs] = acc.astype(BF16)
                continue
            for hh in range(heads):
                sl = slice(hh * HEAD_DIM, (hh + 1) * HEAD_DIM)
                o_ref[rs, sl] = fn(acc[:, sl], rs).astype(BF16)

    is_q = j < q_tiles
    is_k = jnp.logical_and(j >= q_tiles, j < q_tiles + k_tiles)
    is_c = functools.reduce(jnp.logical_or,
                            [jnp.logical_and(j >= lo, j < hi) for lo, hi in rope_c_ranges])
    plain = jnp.logical_not(jnp.logical_or(jnp.logical_or(is_q, is_k), is_c))

    @pl.when(is_q)
    def _():
        project(lambda a, rs: _rope(_rms(a, qg_ref[...]), cosa_ref[rs], sina_ref[rs]))

    @pl.when(is_k)
    def _():
        project(lambda a, rs: _rope(_rms(a, kg_ref[...]), cosa_ref[rs], sina_ref[rs]))

    @pl.when(is_c)
    def _():
        project(lambda a, rs: _rope(a, cosc_ref[rs], sinc_ref[rs]))

    @pl.when(plain)
    def _():
        project(None)


def _inproj(h, w, layer, q_gain, k_gain, tabs, cfg):
    m, d = h.shape
    n = w.shape[2]
    tn, tm = cfg["in_tn"], cfg["mm_tm"]
    tab = pl.BlockSpec((tm, HEAD_DIM), lambda i, j: (i, 0))
    vec = pl.BlockSpec((1, HEAD_DIM), lambda i, j: (0, 0))
    kern = functools.partial(
        _inproj_kernel, tn=tn, q_tiles=cfg["a_q_w"] // tn, k_tiles=cfg["a_kv_w"] // tn,
        rope_c_ranges=[(cfg["off_cq"] // tn, (cfg["off_cq"] + 2 * cfg["c_qk_w"]) // tn)])
    return pl.pallas_call(
        kern,
        grid=(m // tm, n // tn),
        in_specs=[
            pl.BlockSpec((tm, d), lambda i, j: (i, 0)),
            pl.BlockSpec((None, d, tn), lambda i, j: (layer, 0, j)),
            vec, vec, tab, tab, tab, tab,
        ],
        out_specs=pl.BlockSpec((tm, tn), lambda i, j: (i, j)),
        out_shape=jax.ShapeDtypeStruct((m, n), BF16),
        compiler_params=_params("arbitrary", "arbitrary"),
        name="inproj",
    )(h, w, q_gain.reshape(1, HEAD_DIM), k_gain.reshape(1, HEAD_DIM), *tabs)


def _softmax_pv(s, v):
    m = jnp.max(s, axis=-1, keepdims=True)
    p = jnp.exp(s - m)
    l = jnp.sum(p, axis=-1, keepdims=True)
    return _dot(p.astype(BF16), v) / l


ONES_ROWS = 16
EXP_CHUNK = 256


def _attend_t(q, k, vt, c):
    st = _dot_nt(k, q)
    m = jnp.max(st, axis=0, keepdims=True)
    acc = _dot(vt, jnp.exp2((st - m) * c).astype(BF16))
    d = vt.shape[0] - ONES_ROWS
    return acc[:d] / acc[d:d + 1]


def _qk_max(q, k, st_ref):
    st = _dot_nt(k, q)
    st_ref[...] = st
    return jnp.max(st, axis=0, keepdims=True)


def _exp_pv(st_ref, pt_ref, vt, m, c):
    n = st_ref.shape[0]
    for c0 in range(0, n, EXP_CHUNK):
        c1 = min(c0 + EXP_CHUNK, n)
        pt_ref[c0:c1] = jnp.exp2((st_ref[c0:c1] - m) * c).astype(BF16)
    acc = _dot(vt, pt_ref[...])
    d = vt.shape[0] - ONES_ROWS
    return acc[:d] / acc[d:d + 1]


def _attend_tiles(n_iter, n_items, load_q, store_o, k, vt, st_ref, pt_ref, m_ref, c):
    def scores(i, j):
        m_ref[j] = _qk_max(load_q(i, j), k, st_ref.at[j])

    for j in range(n_items - 1):
        scores(0, j)

    def body(i, carry):
        nxt = jnp.minimum(i + 1, n_iter - 1)
        scores(i, n_items - 1)
        for j in range(n_items):
            store_o(i, j, _exp_pv(st_ref.at[j], pt_ref.at[j % 2], vt, m_ref[j], c))
            if j < n_items - 1:
                scores(nxt, j)
        return carry

    lax.fori_loop(0, n_iter, body, 0)


def _fill_vt(vt_ref, v_ref):
    vt_ref[:HEAD_DIM] = v_ref[0].T
    vt_ref[HEAD_DIM:] = jnp.ones((ONES_ROWS, vt_ref.shape[1]), BF16)


def _attn_scratch(t, n_items):
    return [pltpu.VMEM((HEAD_DIM + ONES_ROWS, t), BF16), pltpu.VMEM((n_items, t, ROW_TILE), F32),
            pltpu.VMEM((2, t, ROW_TILE), BF16), pltpu.VMEM((n_items, 1, ROW_TILE), F32)]


def _attn_a_kernel(q_ref, k_ref, v_ref, o_ref, vt_ref, st_ref, pt_ref, m_ref, *, n_ctx, group, c):
    t = k_ref.shape[1]
    tq = ROW_TILE
    _fill_vt(vt_ref, v_ref)
    heads = [slice(g * HEAD_DIM, (g + 1) * HEAD_DIM) for g in range(group)]

    for r0 in range(0, n_ctx, tq):
        for sl in heads:
            o = _attend_t(q_ref[0, r0:r0 + tq, sl], k_ref[0, :n_ctx], vt_ref[:, :n_ctx], c)
            o_ref[0, r0:r0 + tq, sl] = o.T.astype(BF16)

    def rows(i):
        return pl.ds(pl.multiple_of(n_ctx + i * tq, tq), tq)

    def load_q(i, g):
        return q_ref[0, rows(i), heads[g]]

    def store_o(i, g, o):
        o_ref[0, rows(i), heads[g]] = o.T.astype(BF16)

    _attend_tiles((t - n_ctx) // tq, group, load_q, store_o, k_ref[0], vt_ref[...], st_ref, pt_ref, m_ref, c)


def _attn_a(proj, cfg):
    b, t, _ = proj.shape
    gw = Q_GROUP * HEAD_DIM
    kvh = cfg["a_kv_w"] // HEAD_DIM
    k_blk = cfg["a_q_w"] // HEAD_DIM
    v_blk = (cfg["a_q_w"] + cfg["a_kv_w"]) // HEAD_DIM
    kern = functools.partial(_attn_a_kernel, n_ctx=cfg["n_ctx"], group=Q_GROUP, c=HEAD_DIM ** -0.5 * LOG2_E)
    return pl.pallas_call(
        kern,
        grid=(b, kvh),
        in_specs=[
            pl.BlockSpec((1, t, gw), lambda bi, hi: (bi, 0, hi)),
            pl.BlockSpec((1, t, HEAD_DIM), lambda bi, hi: (bi, 0, k_blk + hi)),
            pl.BlockSpec((1, t, HEAD_DIM), lambda bi, hi: (bi, 0, v_blk + hi)),
        ],
        out_specs=pl.BlockSpec((1, t, gw), lambda bi, hi: (bi, 0, hi)),
        out_shape=jax.ShapeDtypeStruct((b, t, cfg["a_q_w"]), BF16),
        scratch_shapes=_attn_scratch(t, Q_GROUP),
        compiler_params=_params("arbitrary", "arbitrary"),
        name="attn_a",
    )(proj, proj, proj)


def _attn_c_kernel(q_ref, k_ref, v_ref, l1_ref, l2_ref, l3_ref, l4_ref, gain_ref, o_ref,
                   vt_ref, st_ref, pt_ref, m_ref, *, n_ctx, c, lam_init):
    t = k_ref.shape[1]
    tq = ROW_TILE
    _fill_vt(vt_ref, v_ref)
    lam = (jnp.exp(jnp.sum(l1_ref[...] * l2_ref[...], axis=-1, keepdims=True))
           - jnp.exp(jnp.sum(l3_ref[...] * l4_ref[...], axis=-1, keepdims=True)) + lam_init)
    lane = lax.broadcasted_iota(jnp.int32, (tq, HEAD_DIM), 1)
    first = lane < HEAD_DIM // 2

    def component(q, comp):
        return jnp.where(first == (comp == 0), q, jnp.zeros_like(q))

    def finish(o1, o2):
        o = (o1 - lam * o2).T
        return (_rms(o, gain_ref[...]) * (1.0 - lam_init)).astype(BF16)

    for r0 in range(0, n_ctx, tq):
        q = q_ref[0, r0:r0 + tq]
        o1, o2 = [_attend_t(component(q, comp), k_ref[0, :n_ctx], vt_ref[:, :n_ctx], c) for comp in range(2)]
        o_ref[0, r0:r0 + tq] = finish(o1, o2)

    def rows(i, j):
        return pl.ds(pl.multiple_of(n_ctx + (2 * i + j // 2) * tq, tq), tq)

    def load_q(i, j):
        return component(q_ref[0, rows(i, j)], j % 2)

    pending = {}

    def store_o(i, j, o):
        if j % 2 == 0:
            pending[0] = o
        else:
            o_ref[0, rows(i, j)] = finish(pending.pop(0), o)

    _attend_tiles((t - n_ctx) // (2 * tq), 4, load_q, store_o, k_ref[0], vt_ref[...], st_ref, pt_ref, m_ref, c)


def _attn_c(proj, lams, subln, lam_init, cfg):
    b, t, _ = proj.shape
    heads = cfg["c_v_w"] // HEAD_DIM
    q_blk = cfg["off_cq"] // HEAD_DIM
    k_blk = q_blk + cfg["c_qk_w"] // HEAD_DIM
    v_blk = k_blk + cfg["c_qk_w"] // HEAD_DIM
    qk_dim = HEAD_DIM // 2
    lam_spec = pl.BlockSpec((1, qk_dim), lambda bi, hi: (0, 0))
    kern = functools.partial(_attn_c_kernel, n_ctx=cfg["n_ctx"], c=qk_dim ** -0.5 * LOG2_E, lam_init=lam_init)
    return pl.pallas_call(
        kern,
        grid=(b, heads),
        in_specs=[
            pl.BlockSpec((1, t, HEAD_DIM), lambda bi, hi: (bi, 0, q_blk + hi)),
            pl.BlockSpec((1, t, HEAD_DIM), lambda bi, hi: (bi, 0, k_blk + hi)),
            pl.BlockSpec((1, t, HEAD_DIM), lambda bi, hi: (bi, 0, v_blk + hi)),
            lam_spec, lam_spec, lam_spec, lam_spec,
            pl.BlockSpec((1, HEAD_DIM), lambda bi, hi: (0, 0)),
        ],
        out_specs=pl.BlockSpec((1, t, HEAD_DIM), lambda bi, hi: (bi, 0, hi)),
        out_shape=jax.ShapeDtypeStruct((b, t, cfg["c_v_w"]), BF16),
        scratch_shapes=_attn_scratch(t, 4),
        compiler_params=_params("arbitrary", "arbitrary"),
        name="attn_c",
    )(proj, proj, proj, *[v.reshape(1, qk_dim) for v in lams], subln.reshape(1, HEAD_DIM))


def _attn_b_kernel(q_ref, k_ref, v_ref, pair_ref, o_ref, bias_ref, *, n_ctx, blocks, pair_idx, q_rows, k_rows,
                   scale):
    @pl.when(pl.program_id(1) == 0)
    def _():
        for case, case_idx in enumerate(pair_idx):
            for a, row_idx in enumerate(case_idx):
                for p, pair in enumerate(row_idx):
                    bias_ref[case, a * GRID_W:(a + 1) * GRID_W, 2 * p * GRID_W:2 * (p + 1) * GRID_W] = (
                        pair_ref[0, pair])

    k_ctx = k_ref[0, :n_ctx]
    v_ctx = v_ref[0, :n_ctx]
    o_ref[0, :n_ctx] = _softmax_pv(_dot_nt(q_ref[0, :n_ctx], k_ctx) * scale, v_ctx).astype(BF16)
    for q0, k0, case in blocks:
        q = q_ref[0, n_ctx + q0:n_ctx + q0 + q_rows]
        k_win = k_ref[0, n_ctx + k0:n_ctx + k0 + k_rows]
        v_win = v_ref[0, n_ctx + k0:n_ctx + k0 + k_rows]
        s_loc = _dot_nt(q, k_win) * scale + bias_ref[case]
        s_ctx = _dot_nt(q, k_ctx) * scale
        m = jnp.maximum(jnp.max(s_loc, axis=-1, keepdims=True), jnp.max(s_ctx, axis=-1, keepdims=True))
        p_loc = jnp.exp(s_loc - m)
        p_ctx = jnp.exp(s_ctx - m)
        l = jnp.sum(p_loc, axis=-1, keepdims=True) + jnp.sum(p_ctx, axis=-1, keepdims=True)
        o = (_dot(p_loc.astype(BF16), v_win) + _dot(p_ctx.astype(BF16), v_ctx)) / l
        o_ref[0, n_ctx + q0:n_ctx + q0 + q_rows] = o.astype(BF16)


def _nbr_plan(seq):
    rows = seq // GRID_W
    kr = min(NA_ROWS, rows)
    q_rows_n = min(NA_ROWS, rows)
    win = min(2 * NA_ROWS, rows)
    cases, blocks = [], []
    for i in range(rows // q_rows_n):
        ws = int(np.clip(i * q_rows_n - kr // 2, 0, rows - win))
        drmap = np.full((q_rows_n, win), -1, np.int64)
        for a in range(q_rows_n):
            r = i * q_rows_n + a
            r0 = int(np.clip(r - kr // 2, 0, rows - kr))
            for w in range(win):
                kr_abs = ws + w
                if r0 <= kr_abs < r0 + kr:
                    drmap[a, w] = kr_abs - r + (NA_ROWS - 1)
        key = drmap.tobytes()
        keys = [c.tobytes() for c in cases]
        if key not in keys:
            cases.append(drmap)
            keys.append(key)
        blocks.append((i * q_rows_n * GRID_W, ws * GRID_W, keys.index(key)))
    masked = 2 * NA_ROWS - 1
    case_maps = np.where(np.stack(cases) < 0, masked, np.stack(cases))
    pair_list = sorted({(int(d0), int(d1)) for d0, d1 in case_maps.reshape(-1, 2)})
    pair_idx = [[[pair_list.index((int(row[2 * p]), int(row[2 * p + 1]))) for p in range(win // 2)]
                 for row in case] for case in case_maps]
    return blocks, np.asarray(pair_list), pair_idx, q_rows_n * GRID_W, win * GRID_W


def _nbr_bias(rpb, pairs):
    depth, heads = rpb.shape[:2]
    c = np.arange(GRID_W)
    c0 = np.clip(c - NA_COLS // 2, 0, GRID_W - NA_COLS)
    col_mask = (c[None, :] >= c0[:, None]) & (c[None, :] < c0[:, None] + NA_COLS)
    col_idx = np.clip(c[None, :] - c[:, None] + NA_COLS - 1, 0, 2 * NA_COLS - 2)
    toe = jnp.where(col_mask, rpb.astype(F32)[:, :, :, col_idx], -jnp.inf)
    toe = jnp.concatenate([toe, jnp.full((depth, heads, 1, GRID_W, GRID_W), -jnp.inf, F32)], axis=2)
    return jnp.concatenate([jnp.take(toe, pairs[:, 0], axis=2), jnp.take(toe, pairs[:, 1], axis=2)], axis=-1)


def _attn_b(proj, pair_bias, layer, plan, cfg):
    b, t, _ = proj.shape
    heads = cfg["b_w"] // HEAD_DIM
    q_blk = cfg["off_bq"] // HEAD_DIM
    k_blk = q_blk + heads
    v_blk = k_blk + heads
    blocks, _, pair_idx, q_rows, k_rows = plan
    n_pair = pair_bias.shape[2]
    kern = functools.partial(_attn_b_kernel, n_ctx=cfg["n_ctx"], blocks=blocks, pair_idx=pair_idx,
                             q_rows=q_rows, k_rows=k_rows, scale=HEAD_DIM ** -0.5)
    return pl.pallas_call(
        kern,
        grid=(heads, b),
        in_specs=[
            pl.BlockSpec((1, t, HEAD_DIM), lambda hi, bi: (bi, 0, q_blk + hi)),
            pl.BlockSpec((1, t, HEAD_DIM), lambda hi, bi: (bi, 0, k_blk + hi)),
            pl.BlockSpec((1, t, HEAD_DIM), lambda hi, bi: (bi, 0, v_blk + hi)),
            pl.BlockSpec((None, 1, n_pair, GRID_W, 2 * GRID_W), lambda hi, bi: (layer, hi, 0, 0, 0)),
        ],
        out_specs=pl.BlockSpec((1, t, HEAD_DIM), lambda hi, bi: (bi, 0, hi)),
        out_shape=jax.ShapeDtypeStruct((b, t, cfg["b_w"]), BF16),
        scratch_shapes=[pltpu.VMEM((len(pair_idx), q_rows, k_rows), F32)],
        compiler_params=_params("arbitrary", "arbitrary"),
        name="attn_b",
    )(proj, proj, proj, pair_bias)


def _merge_kernel(g_ref, oa_ref, ob_ref, oc_ref, wga_ref, wgb_ref, wgc_ref, bga_ref, bgb_ref, bgc_ref,
                  wa_ref, wb_ref, wc_ref, o_ref):
    g_low = g_ref[...]
    acc = None
    for o_br, w_br, wg, bg in ((oa_ref, wa_ref, wga_ref, bga_ref), (ob_ref, wb_ref, wgb_ref, bgb_ref),
                               (oc_ref, wc_ref, wgc_ref, bgc_ref)):
        gate = jax.nn.sigmoid(_dot(g_low, wg[...].astype(BF16)) + bg[...])
        term = gate * _dot(o_br[...], w_br[...].astype(BF16))
        acc = term if acc is None else acc + term
    o_ref[...] = acc.astype(BF16)


def _merge(proj2, o_a, o_b, o_c, w_gate, b_gate, w_branch, layer, cfg):
    m = proj2.shape[0]
    d = w_branch.shape[2]
    rank = w_gate.shape[1]
    tm, tn = cfg["mm_tm"], cfg["mm_tn"]
    a_w, b_w, c_w = cfg["a_q_w"], cfg["b_w"], cfg["c_v_w"]
    g_blk = cfg["off_g"] // rank
    nb = d // tn
    gate_specs = [pl.BlockSpec((None, rank, tn), lambda i, j, br=br: (layer, 0, br * nb + j)) for br in range(3)]
    bias_specs = [pl.BlockSpec((None, 1, tn), lambda i, j, br=br: (layer, 0, br * nb + j)) for br in range(3)]
    return pl.pallas_call(
        _merge_kernel,
        grid=(m // tm, nb),
        in_specs=[
            pl.BlockSpec((tm, rank), lambda i, j: (i, g_blk)),
            pl.BlockSpec((tm, a_w), lambda i, j: (i, 0)),
            pl.BlockSpec((tm, b_w), lambda i, j: (i, 0)),
            pl.BlockSpec((tm, c_w), lambda i, j: (i, 0)),
            *gate_specs, *bias_specs,
            pl.BlockSpec((None, a_w, tn), lambda i, j: (layer, 0, j)),
            pl.BlockSpec((None, b_w, tn), lambda i, j: (layer, a_w // b_w, j)),
            pl.BlockSpec((None, c_w, tn), lambda i, j: (layer, (a_w + b_w) // c_w, j)),
        ],
        out_specs=pl.BlockSpec((tm, tn), lambda i, j: (i, j)),
        out_shape=jax.ShapeDtypeStruct((m, d), BF16),
        compiler_params=_params("arbitrary", "arbitrary"),
        name="merge",
    )(proj2, o_a, o_b, o_c, w_gate, w_gate, w_gate, b_gate, b_gate, b_gate, w_branch, w_branch, w_branch)


def _matmul_kernel(a_ref, w_ref, o_ref):
    w = w_ref[...]
    o_ref[...] = _dot(a_ref[...], w if w.dtype == BF16 else w.astype(BF16))


def _matmul_f32(a, w, layer, tm, tn, name):
    m, k = a.shape
    n = w.shape[2]
    return pl.pallas_call(
        _matmul_kernel,
        grid=(m // tm, n // tn),
        in_specs=[pl.BlockSpec((tm, k), lambda i, j: (i, 0)),
                  pl.BlockSpec((None, k, tn), lambda i, j: (layer, 0, j))],
        out_specs=pl.BlockSpec((tm, tn), lambda i, j: (i, j)),
        out_shape=jax.ShapeDtypeStruct((m, n), F32),
        compiler_params=_params("arbitrary", "arbitrary"),
        name=name,
    )(a, w)


def _ffn_in_kernel(h_ref, wg_ref, wu_ref, o_ref):
    h = h_ref[...]
    gate = _dot(h, wg_ref[...].astype(BF16))
    up = _dot(h, wu_ref[...].astype(BF16))
    o_ref[...] = (gate * jax.nn.sigmoid(gate) * up).astype(BF16)


def _ffn_in(h, w, layer, cfg):
    m, d = h.shape
    d_ff = w.shape[2] // 2
    tm, tn = cfg["mm_tm"], cfg["ffn_tn"]
    up_blk = d_ff // tn
    return pl.pallas_call(
        _ffn_in_kernel,
        grid=(m // tm, d_ff // tn),
        in_specs=[pl.BlockSpec((tm, d), lambda i, j: (i, 0)),
                  pl.BlockSpec((None, d, tn), lambda i, j: (layer, 0, j)),
                  pl.BlockSpec((None, d, tn), lambda i, j: (layer, 0, up_blk + j))],
        out_specs=pl.BlockSpec((tm, tn), lambda i, j: (i, j)),
        out_shape=jax.ShapeDtypeStruct((m, d_ff), BF16),
        compiler_params=_params("arbitrary", "arbitrary"),
        name="ffn_in",
    )(h, w, w)


def _rope_tables(seq, n_ctx, batch, dim):
    t = np.arange(seq)
    rows = (t // GRID_W).astype(np.float32)
    cols = (t % GRID_W).astype(np.float32)
    n_pairs = dim // 4
    freqs = jnp.asarray(ROPE_THETA, F32) ** (-jnp.arange(n_pairs, dtype=F32) / n_pairs)
    ang = jnp.concatenate([jnp.asarray(rows)[:, None] * freqs, jnp.asarray(cols)[:, None] * freqs], axis=-1)
    sign = jnp.asarray(np.tile([-1.0, 1.0], dim // 2), F32)
    cos = jnp.tile(jnp.repeat(jnp.cos(ang), 2, axis=1), (1, HEAD_DIM // dim))
    sin = jnp.tile(jnp.repeat(jnp.sin(ang), 2, axis=1) * sign, (1, HEAD_DIM // dim))
    cos = jnp.concatenate([jnp.ones((n_ctx, HEAD_DIM), F32), cos], axis=0)
    sin = jnp.concatenate([jnp.zeros((n_ctx, HEAD_DIM), F32), sin], axis=0)
    return jnp.tile(cos, (batch, 1)), jnp.tile(sin, (batch, 1))


def _config(d, n_ctx, in_w, rank, d_ff):
    heads = d // HEAD_DIM
    a_q_w = (heads // 2) * HEAD_DIM
    a_kv_w = a_q_w // Q_GROUP
    b_w = (heads // 4) * HEAD_DIM
    c_qk_w = c_v_w = (heads // 4) * HEAD_DIM
    off_bq = a_q_w + 2 * a_kv_w
    off_cq = off_bq + 3 * b_w
    off_g = off_cq + 2 * c_qk_w + c_v_w
    assert off_g + rank == in_w and off_g % rank == 0
    assert a_q_w % b_w == 0 and (a_q_w + b_w) % c_v_w == 0
    in_tn = math.gcd(math.gcd(a_kv_w, b_w), 512)
    ffn_tn = math.gcd(d_ff, 512)
    return dict(n_ctx=n_ctx, a_q_w=a_q_w, a_kv_w=a_kv_w, b_w=b_w, c_qk_w=c_qk_w, c_v_w=c_v_w,
                off_bq=off_bq, off_cq=off_cq, off_g=off_g, in_tn=in_tn, ffn_tn=ffn_tn,
                mm_tn=min(512, d))


def kernel(x, c, ctx, c_ctx, ada_down, ada_up, ada_bias, norm_mix_pre, norm_mix_post, norm_ffn_pre,
           norm_ffn_post, w_in, a_q_norm, a_k_norm, b_rel_bias, c_lambda_q1, c_lambda_k1, c_lambda_q2,
           c_lambda_k2, c_subln, w_gate_up, b_gate, w_branch, w_out, w_ffn_in, w_ffn_out):
    batch, seq, d = x.shape
    n_ctx = ctx.shape[1]
    depth, _, in_w = w_in.shape
    rank = w_gate_up.shape[1]
    d_ff = w_ffn_out.shape[1]
    t = n_ctx + seq
    m = batch * t
    cfg = _config(d, n_ctx, in_w, rank, d_ff)
    assert n_ctx % ROW_TILE == 0 and seq % ROW_TILE == 0 and seq % (NA_ROWS * GRID_W) == 0
    cfg["mm_tm"] = math.gcd(m, MM_TILE_M)
    geom = (t // ROW_TILE, n_ctx // ROW_TILE, batch)

    w_ffn_out_b = w_ffn_out.astype(BF16)
    b_gate_r = b_gate.reshape(depth, 1, 3 * d)

    tabs = _rope_tables(seq, n_ctx, batch, HEAD_DIM) + _rope_tables(seq, n_ctx, batch, HEAD_DIM // 2)
    plan = _nbr_plan(seq)
    nbr_bias = _nbr_bias(b_rel_bias, plan[1])

    rows = -(-(batch + 1) // 8) * 8
    cond = jnp.zeros((rows, d), F32).at[:batch].set(c).at[batch].set(c_ctx)
    mods = _adaln(cond, ada_down, ada_up, ada_bias).reshape(depth, 6, rows, 1, d)

    xs = jnp.concatenate([ctx, x], axis=1).reshape(m, d)
    h = _prenorm(xs, norm_mix_pre[0], mods, 0, 0, 1, geom)
    for l in range(depth):
        lam_init = 0.8 - 0.6 * math.exp(-0.3 * l)
        proj = _inproj(h, w_in, l, a_q_norm[l], a_k_norm[l], tabs, cfg)
        proj3 = proj.reshape(batch, t, in_w)
        o_a = _attn_a(proj3, cfg).reshape(m, -1)
        o_b = _attn_b(proj3, nbr_bias, l, plan, cfg).reshape(m, -1)
        o_c = _attn_c(proj3, (c_lambda_q1[l], c_lambda_k1[l], c_lambda_q2[l], c_lambda_k2[l]),
                      c_subln[l], lam_init, cfg).reshape(m, -1)
        merged = _merge(proj, o_a, o_b, o_c, w_gate_up, b_gate_r, w_branch, l, cfg)
        mix = _matmul_f32(merged, w_out, l, cfg["mm_tm"], cfg["mm_tn"], "out_proj")
        xs, h = _resid(xs, mix, norm_mix_post[l], norm_ffn_pre[l], mods, l, 2, l, 3, 4, geom)
        act = _ffn_in(h, w_ffn_in, l, cfg)
        f = _matmul_f32(act, w_ffn_out_b, l, min(512, cfg["mm_tm"]), cfg["mm_tn"], "ffn_out")
        nxt = min(l + 1, depth - 1)
        xs, h = _resid(xs, f, norm_ffn_post[l], norm_mix_pre[nxt], mods, l, 5, nxt, 0, 1, geom)
    return xs.reshape(batch, t, d)[:, n_ctx:]
```

```python
import functools
import math

import jax
import jax.numpy as jnp
import numpy as np
from jax import lax
from jax.experimental import pallas as pl
from jax.experimental.pallas import tpu as pltpu

HEAD_DIM = 128
GRID_W = 64
NA_ROWS = 8
NA_COLS = 16
Q_GROUP = 4
ROPE_THETA = 10000.0
EPS = 1e-6
LOG2_E = math.log2(math.e)
ROW_TILE = 256
MM_TILE_M = 1024
MM_TILE_M_BIG = 2304
SUB_TILES = 8
ROW_ALIGN = 128
VMEM_LIMIT = 56 * 1024 * 1024
BF16 = jnp.bfloat16
F32 = jnp.float32


def _params(*sem):
    return pltpu.CompilerParams(dimension_semantics=sem, vmem_limit_bytes=VMEM_LIMIT)


def _dot(a, b):
    return jnp.dot(a, b, preferred_element_type=F32)


def _dot_nt(a, b):
    return lax.dot_general(a, b, (((1,), (1,)), ((), ())), preferred_element_type=F32)


def _rms(x, gain):
    return x * lax.rsqrt(jnp.mean(x * x, axis=-1, keepdims=True) + EPS) * gain


def _adaln_kernel(cond_ref, down_ref, up_ref, bias_ref, o_ref):
    cond = cond_ref[...]
    act = (cond * jax.nn.sigmoid(cond)).astype(BF16)
    low = _dot(act, down_ref[0].astype(BF16))
    o_ref[0, 0] = _dot(low.astype(BF16), up_ref[0].astype(BF16)) + bias_ref[0]


def _adaln(cond, ada_down, ada_up, ada_bias):
    depth, d, rank = ada_down.shape
    rows = cond.shape[0]
    return pl.pallas_call(
        _adaln_kernel,
        grid=(depth, 6),
        in_specs=[
            pl.BlockSpec((rows, d), lambda l, k: (0, 0)),
            pl.BlockSpec((1, d, rank), lambda l, k: (l, 0, 0)),
            pl.BlockSpec((1, rank, d), lambda l, k: (l, 0, k)),
            pl.BlockSpec((1, 1, d), lambda l, k: (l, 0, k)),
        ],
        out_specs=pl.BlockSpec((1, 1, rows, d), lambda l, k: (l, k, 0, 0)),
        out_shape=jax.ShapeDtypeStruct((depth, 6, rows, d), F32),
        compiler_params=_params("arbitrary", "arbitrary"),
        name="adaln",
    )(cond, ada_down, ada_up, ada_bias.reshape(depth, 1, 6 * d))


def _mod_row(i, tiles_per_batch, ctx_tiles, batch):
    return jnp.where(i % tiles_per_batch < ctx_tiles, batch, i // tiles_per_batch)


def _embed_kernel(x_ref, ctx_ref, gain_ref, shift_ref, scale_ref, xs_ref, h_ref, *, tiles_per_batch, ctx_tiles):
    def emit(src):
        xs_ref[...] = src
        h_ref[...] = (_rms(src, gain_ref[...]) * (1.0 + scale_ref[0, 0, 0]) + shift_ref[0, 0, 0]).astype(BF16)

    is_ctx = pl.program_id(0) % tiles_per_batch < ctx_tiles

    @pl.when(is_ctx)
    def _():
        emit(ctx_ref[0])

    @pl.when(jnp.logical_not(is_ctx))
    def _():
        emit(x_ref[0])


def _embed(x, ctx, gain, mods, geom):
    batch, seq, d = x.shape
    tpb, ctx_tiles, _ = geom
    m = batch * tpb * ROW_TILE
    row = functools.partial(_mod_row, tiles_per_batch=tpb, ctx_tiles=ctx_tiles, batch=batch)
    tile = pl.BlockSpec((ROW_TILE, d), lambda i: (i, 0))
    kern = functools.partial(_embed_kernel, tiles_per_batch=tpb, ctx_tiles=ctx_tiles)
    return pl.pallas_call(
        kern,
        grid=(m // ROW_TILE,),
        in_specs=[
            pl.BlockSpec((1, ROW_TILE, d), lambda i: (i // tpb, jnp.maximum(i % tpb - ctx_tiles, 0), 0)),
            pl.BlockSpec((1, ROW_TILE, d), lambda i: (i // tpb, jnp.minimum(i % tpb, ctx_tiles - 1), 0)),
            pl.BlockSpec((1, d), lambda i: (0, 0)),
            pl.BlockSpec((1, 1, 1, 1, d), lambda i: (0, 0, row(i), 0, 0)),
            pl.BlockSpec((1, 1, 1, 1, d), lambda i: (0, 1, row(i), 0, 0)),
        ],
        out_specs=[tile, tile],
        out_shape=[jax.ShapeDtypeStruct((m, d), F32), jax.ShapeDtypeStruct((m, d), BF16)],
        compiler_params=_params("arbitrary"),
        name="embed",
    )(x, ctx, gain.reshape(1, d), mods, mods)


def _resid_kernel(x_ref, y_ref, post_ref, gate_ref, pre_ref, shift_ref, scale_ref, xo_ref, h_ref):
    x_new = x_ref[...] + gate_ref[0, 0, 0] * _rms(y_ref[...], post_ref[...])
    xo_ref[...] = x_new
    h = _rms(x_new, pre_ref[...])
    h_ref[...] = (h * (1.0 + scale_ref[0, 0, 0]) + shift_ref[0, 0, 0]).astype(BF16)


def _resid(xs, y, post_gain, pre_gain, mods, layer, k_gate, mod_layer, k_shift, k_scale, geom):
    m, d = xs.shape
    tpb, ctx_tiles, batch = geom
    row = functools.partial(_mod_row, tiles_per_batch=tpb, ctx_tiles=ctx_tiles, batch=batch)
    tile = pl.BlockSpec((ROW_TILE, d), lambda i: (i, 0))
    vec = pl.BlockSpec((1, d), lambda i: (0, 0))
    return pl.pallas_call(
        _resid_kernel,
        grid=(m // ROW_TILE,),
        in_specs=[
            tile, tile, vec,
            pl.BlockSpec((1, 1, 1, 1, d), lambda i: (layer, k_gate, row(i), 0, 0)),
            vec,
            pl.BlockSpec((1, 1, 1, 1, d), lambda i: (mod_layer, k_shift, row(i), 0, 0)),
            pl.BlockSpec((1, 1, 1, 1, d), lambda i: (mod_layer, k_scale, row(i), 0, 0)),
        ],
        out_specs=[tile, tile],
        out_shape=[jax.ShapeDtypeStruct((m, d), F32), jax.ShapeDtypeStruct((m, d), BF16)],
        compiler_params=_params("arbitrary"),
        name="resid",
    )(xs, y, post_gain.reshape(1, d), mods, pre_gain.reshape(1, d), mods, mods)


def _resid_out_kernel(x_ref, y_ref, post_ref, gate_ref, o_ref):
    o_ref[...] = x_ref[...] + gate_ref[0, 0, 0] * _rms(y_ref[...], post_ref[...])


def _resid_out(xs, y, post_gain, mods, layer, k_gate, geom):
    m, d = xs.shape
    tpb, ctx_tiles, batch = geom
    lat = tpb - ctx_tiles
    row = functools.partial(_mod_row, tiles_per_batch=tpb, ctx_tiles=ctx_tiles, batch=batch)
    tile = pl.BlockSpec((ROW_TILE, d), lambda i: (i, 0))
    return pl.pallas_call(
        _resid_out_kernel,
        grid=(m // ROW_TILE,),
        in_specs=[
            tile, tile, pl.BlockSpec((1, d), lambda i: (0, 0)),
            pl.BlockSpec((1, 1, 1, 1, d), lambda i: (layer, k_gate, row(i), 0, 0)),
        ],
        out_specs=pl.BlockSpec((ROW_TILE, d), lambda i: ((i // tpb) * lat + jnp.maximum(i % tpb - ctx_tiles, 0), 0)),
        out_shape=jax.ShapeDtypeStruct((batch * lat * ROW_TILE, d), F32),
        compiler_params=_params("arbitrary"),
        name="resid_out",
    )(xs, y, post_gain.reshape(1, d), mods)


def _rope(y, cos, sin):
    lane = lax.broadcasted_iota(jnp.int32, y.shape, 1)
    partner = jnp.where(lane % 2 == 0, pltpu.roll(y, HEAD_DIM - 1, 1), pltpu.roll(y, 1, 1))
    return y * cos + partner * sin


def _inproj_kernel(h_ref, w_ref, qg_ref, kg_ref, cosa_ref, sina_ref, cosc_ref, sinc_ref, o_ref,
                   *, tn, q_tiles, k_tiles, rope_c_ranges):
    j = pl.program_id(1)
    heads = tn // HEAD_DIM
    n_sub = max(k for k in range(1, SUB_TILES + 1) if h_ref.shape[0] % (16 * k) == 0)
    sub_rows = h_ref.shape[0] // n_sub

    def project(fn):
        for s in range(n_sub):
            rs = slice(s * sub_rows, (s + 1) * sub_rows)
            acc = _dot(h_ref[rs], w_ref[...].astype(BF16))
            if fn is None:
                o_ref[rs] = acc.astype(BF16)
                continue
            for hh in range(heads):
                sl = slice(hh * HEAD_DIM, (hh + 1) * HEAD_DIM)
                o_ref[rs, sl] = fn(acc[:, sl], rs).astype(BF16)

    is_q = j < q_tiles
    is_k = jnp.logical_and(j >= q_tiles, j < q_tiles + k_tiles)
    is_c = functools.reduce(jnp.logical_or,
                            [jnp.logical_and(j >= lo, j < hi) for lo, hi in rope_c_ranges])
    plain = jnp.logical_not(jnp.logical_or(jnp.logical_or(is_q, is_k), is_c))

    @pl.when(is_q)
    def _():
        project(lambda a, rs: _rope(_rms(a, qg_ref[...]), cosa_ref[rs], sina_ref[rs]))

    @pl.when(is_k)
    def _():
        project(lambda a, rs: _rope(_rms(a, kg_ref[...]), cosa_ref[rs], sina_ref[rs]))

    @pl.when(is_c)
    def _():
        project(lambda a, rs: _rope(a, cosc_ref[rs], sinc_ref[rs]))

    @pl.when(plain)
    def _():
        project(None)


def _inproj(h, w, layer, q_gain, k_gain, tabs, cfg):
    m, d = h.shape
    n = w.shape[2]
    tn, tm = cfg["in_tn"], cfg["big_tm"]
    tab = pl.BlockSpec((tm, HEAD_DIM), lambda i, j: (i, 0), pipeline_mode=pl.Buffered(1))
    vec = pl.BlockSpec((1, HEAD_DIM), lambda i, j: (0, 0))
    kern = functools.partial(
        _inproj_kernel, tn=tn, q_tiles=cfg["a_q_w"] // tn, k_tiles=cfg["a_kv_w"] // tn,
        rope_c_ranges=[(cfg["off_cq"] // tn, (cfg["off_cq"] + 2 * cfg["c_qk_w"]) // tn)])
    return pl.pallas_call(
        kern,
        grid=(m // tm, n // tn),
        in_specs=[
            pl.BlockSpec((tm, d), lambda i, j: (i, 0), pipeline_mode=pl.Buffered(1)),
            pl.BlockSpec((None, d, tn), lambda i, j: (layer, 0, j)),
            vec, vec, tab, tab, tab, tab,
        ],
        out_specs=pl.BlockSpec((tm, tn), lambda i, j: (i, j)),
        out_shape=jax.ShapeDtypeStruct((m, n), BF16),
        compiler_params=_params("arbitrary", "arbitrary"),
        name="inproj",
    )(h, w, q_gain.reshape(1, HEAD_DIM), k_gain.reshape(1, HEAD_DIM), *tabs)


def _softmax_pv(s, v):
    m = jnp.max(s, axis=-1, keepdims=True)
    p = jnp.exp(s - m)
    l = jnp.sum(p, axis=-1, keepdims=True)
    return _dot(p.astype(BF16), v) / l


ONES_ROWS = 16
EXP_CHUNK = 256


def _attend_t(q, k, vt, c):
    st = _dot_nt(k, q)
    m = jnp.max(st, axis=0, keepdims=True)
    acc = _dot(vt, jnp.exp2((st - m) * c).astype(BF16))
    d = vt.shape[0] - ONES_ROWS
    return acc[:d] / acc[d:d + 1]


def _qk_max(q, k, st_ref):
    st = _dot_nt(k, q)
    st_ref[...] = st
    return jnp.max(st, axis=0, keepdims=True)


def _exp_pv(st_ref, pt_ref, vt, m, c):
    n = st_ref.shape[0]
    for c0 in range(0, n, EXP_CHUNK):
        c1 = min(c0 + EXP_CHUNK, n)
        pt_ref[c0:c1] = jnp.exp2((st_ref[c0:c1] - m) * c).astype(BF16)
    acc = _dot(vt, pt_ref[...])
    d = vt.shape[0] - ONES_ROWS
    return acc[:d] / acc[d:d + 1]


def _attend_tiles(n_iter, n_items, load_q, store_o, k, vt, st_ref, pt_ref, m_ref, c):
    def scores(i, j):
        m_ref[j] = _qk_max(load_q(i, j), k, st_ref.at[j])

    for j in range(n_items - 1):
        scores(0, j)

    def body(i, carry):
        nxt = jnp.minimum(i + 1, n_iter - 1)
        scores(i, n_items - 1)
        for j in range(n_items):
            store_o(i, j, _exp_pv(st_ref.at[j], pt_ref.at[j % 2], vt, m_ref[j], c))
            if j < n_items - 1:
                scores(nxt, j)
        return carry

    lax.fori_loop(0, n_iter, body, 0)


def _fill_vt(vt_ref, v_ref):
    vt_ref[:HEAD_DIM] = v_ref[0].T
    vt_ref[HEAD_DIM:] = jnp.ones((ONES_ROWS, vt_ref.shape[1]), BF16)


def _attn_scratch(t, n_items):
    return [pltpu.VMEM((HEAD_DIM + ONES_ROWS, t), BF16), pltpu.VMEM((n_items, t, ROW_TILE), F32),
            pltpu.VMEM((2, t, ROW_TILE), BF16), pltpu.VMEM((n_items, 1, ROW_TILE), F32)]


def _attn_a_kernel(q_ref, k_ref, v_ref, o_ref, vt_ref, st_ref, pt_ref, m_ref, *, n_ctx, group, c):
    t = k_ref.shape[1]
    tq = ROW_TILE
    _fill_vt(vt_ref, v_ref)
    heads = [slice(g * HEAD_DIM, (g + 1) * HEAD_DIM) for g in range(group)]

    for r0 in range(0, n_ctx, tq):
        for sl in heads:
            o = _attend_t(q_ref[0, r0:r0 + tq, sl], k_ref[0, :n_ctx], vt_ref[:, :n_ctx], c)
            o_ref[0, r0:r0 + tq, sl] = o.T.astype(BF16)

    def rows(i):
        return pl.ds(pl.multiple_of(n_ctx + i * tq, tq), tq)

    def load_q(i, g):
        return q_ref[0, rows(i), heads[g]]

    def store_o(i, g, o):
        o_ref[0, rows(i), heads[g]] = o.T.astype(BF16)

    _attend_tiles((t - n_ctx) // tq, group, load_q, store_o, k_ref[0], vt_ref[...], st_ref, pt_ref, m_ref, c)


def _attn_a(proj, cfg):
    b, t, _ = proj.shape
    gw = Q_GROUP * HEAD_DIM
    kvh = cfg["a_kv_w"] // HEAD_DIM
    k_blk = cfg["a_q_w"] // HEAD_DIM
    v_blk = (cfg["a_q_w"] + cfg["a_kv_w"]) // HEAD_DIM
    kern = functools.partial(_attn_a_kernel, n_ctx=cfg["n_ctx"], group=Q_GROUP, c=HEAD_DIM ** -0.5 * LOG2_E)
    return pl.pallas_call(
        kern,
        grid=(b, kvh),
        in_specs=[
            pl.BlockSpec((1, t, gw), lambda bi, hi: (bi, 0, hi)),
            pl.BlockSpec((1, t, HEAD_DIM), lambda bi, hi: (bi, 0, k_blk + hi)),
            pl.BlockSpec((1, t, HEAD_DIM), lambda bi, hi: (bi, 0, v_blk + hi)),
        ],
        out_specs=pl.BlockSpec((1, t, gw), lambda bi, hi: (bi, 0, hi)),
        out_shape=jax.ShapeDtypeStruct((b, t, cfg["a_q_w"]), BF16),
        scratch_shapes=_attn_scratch(t, Q_GROUP),
        compiler_params=_params("arbitrary", "arbitrary"),
        name="attn_a",
    )(proj, proj, proj)


def _attn_c_kernel(q_ref, k_ref, v_ref, l1_ref, l2_ref, l3_ref, l4_ref, gain_ref, o_ref,
                   vt_ref, st_ref, pt_ref, m_ref, *, n_ctx, c, lam_init):
    t = k_ref.shape[1]
    tq = ROW_TILE
    _fill_vt(vt_ref, v_ref)
    lam = (jnp.exp(jnp.sum(l1_ref[...] * l2_ref[...], axis=-1, keepdims=True))
           - jnp.exp(jnp.sum(l3_ref[...] * l4_ref[...], axis=-1, keepdims=True)) + lam_init)
    lane = lax.broadcasted_iota(jnp.int32, (tq, HEAD_DIM), 1)
    first = lane < HEAD_DIM // 2

    def component(q, comp):
        return jnp.where(first == (comp == 0), q, jnp.zeros_like(q))

    def finish(o1, o2):
        o = (o1 - lam * o2).T
        return (_rms(o, gain_ref[...]) * (1.0 - lam_init)).astype(BF16)

    for r0 in range(0, n_ctx, tq):
        q = q_ref[0, r0:r0 + tq]
        o1, o2 = [_attend_t(component(q, comp), k_ref[0, :n_ctx], vt_ref[:, :n_ctx], c) for comp in range(2)]
        o_ref[0, r0:r0 + tq] = finish(o1, o2)

    def rows(i, j):
        return pl.ds(pl.multiple_of(n_ctx + (2 * i + j // 2) * tq, tq), tq)

    def load_q(i, j):
        return component(q_ref[0, rows(i, j)], j % 2)

    pending = {}

    def store_o(i, j, o):
        if j % 2 == 0:
            pending[0] = o
        else:
            o_ref[0, rows(i, j)] = finish(pending.pop(0), o)

    _attend_tiles((t - n_ctx) // (2 * tq), 4, load_q, store_o, k_ref[0], vt_ref[...], st_ref, pt_ref, m_ref, c)


def _attn_c(proj, lams, subln, lam_init, cfg):
    b, t, _ = proj.shape
    heads = cfg["c_v_w"] // HEAD_DIM
    q_blk = cfg["off_cq"] // HEAD_DIM
    k_blk = q_blk + cfg["c_qk_w"] // HEAD_DIM
    v_blk = k_blk + cfg["c_qk_w"] // HEAD_DIM
    qk_dim = HEAD_DIM // 2
    lam_spec = pl.BlockSpec((1, qk_dim), lambda bi, hi: (0, 0))
    kern = functools.partial(_attn_c_kernel, n_ctx=cfg["n_ctx"], c=qk_dim ** -0.5 * LOG2_E, lam_init=lam_init)
    return pl.pallas_call(
        kern,
        grid=(b, heads),
        in_specs=[
            pl.BlockSpec((1, t, HEAD_DIM), lambda bi, hi: (bi, 0, q_blk + hi)),
            pl.BlockSpec((1, t, HEAD_DIM), lambda bi, hi: (bi, 0, k_blk + hi)),
            pl.BlockSpec((1, t, HEAD_DIM), lambda bi, hi: (bi, 0, v_blk + hi)),
            lam_spec, lam_spec, lam_spec, lam_spec,
            pl.BlockSpec((1, HEAD_DIM), lambda bi, hi: (0, 0)),
        ],
        out_specs=pl.BlockSpec((1, t, HEAD_DIM), lambda bi, hi: (bi, 0, hi)),
        out_shape=jax.ShapeDtypeStruct((b, t, cfg["c_v_w"]), BF16),
        scratch_shapes=_attn_scratch(t, 4),
        compiler_params=_params("arbitrary", "arbitrary"),
        name="attn_c",
    )(proj, proj, proj, *[v.reshape(1, qk_dim) for v in lams], subln.reshape(1, HEAD_DIM))


def _attn_b_kernel(q_ref, k_ref, v_ref, pair_ref, o_ref, bias_ref, *, n_ctx, blocks, pair_idx, q_rows, k_rows,
                   scale):
    @pl.when(pl.program_id(1) == 0)
    def _():
        for case, case_idx in enumerate(pair_idx):
            for a, row_idx in enumerate(case_idx):
                for p, pair in enumerate(row_idx):
                    bias_ref[case, a * GRID_W:(a + 1) * GRID_W, 2 * p * GRID_W:2 * (p + 1) * GRID_W] = (
                        pair_ref[0, pair])

    k_ctx = k_ref[0, :n_ctx]
    v_ctx = v_ref[0, :n_ctx]
    o_ref[0, :n_ctx] = _softmax_pv(_dot_nt(q_ref[0, :n_ctx], k_ctx) * scale, v_ctx).astype(BF16)
    for q0, k0, case in blocks:
        q = q_ref[0, n_ctx + q0:n_ctx + q0 + q_rows]
        k_win = k_ref[0, n_ctx + k0:n_ctx + k0 + k_rows]
        v_win = v_ref[0, n_ctx + k0:n_ctx + k0 + k_rows]
        s_loc = _dot_nt(q, k_win) * scale + bias_ref[case]
        s_ctx = _dot_nt(q, k_ctx) * scale
        m = jnp.maximum(jnp.max(s_loc, axis=-1, keepdims=True), jnp.max(s_ctx, axis=-1, keepdims=True))
        p_loc = jnp.exp(s_loc - m)
        p_ctx = jnp.exp(s_ctx - m)
        l = jnp.sum(p_loc, axis=-1, keepdims=True) + jnp.sum(p_ctx, axis=-1, keepdims=True)
        o = (_dot(p_loc.astype(BF16), v_win) + _dot(p_ctx.astype(BF16), v_ctx)) / l
        o_ref[0, n_ctx + q0:n_ctx + q0 + q_rows] = o.astype(BF16)


def _nbr_plan(seq):
    rows = seq // GRID_W
    kr = min(NA_ROWS, rows)
    q_rows_n = min(NA_ROWS, rows)
    win = min(2 * NA_ROWS, rows)
    cases, blocks = [], []
    for i in range(rows // q_rows_n):
        ws = int(np.clip(i * q_rows_n - kr // 2, 0, rows - win))
        drmap = np.full((q_rows_n, win), -1, np.int64)
        for a in range(q_rows_n):
            r = i * q_rows_n + a
            r0 = int(np.clip(r - kr // 2, 0, rows - kr))
            for w in range(win):
                kr_abs = ws + w
                if r0 <= kr_abs < r0 + kr:
                    drmap[a, w] = kr_abs - r + (NA_ROWS - 1)
        key = drmap.tobytes()
        keys = [c.tobytes() for c in cases]
        if key not in keys:
            cases.append(drmap)
            keys.append(key)
        blocks.append((i * q_rows_n * GRID_W, ws * GRID_W, keys.index(key)))
    masked = 2 * NA_ROWS - 1
    case_maps = np.where(np.stack(cases) < 0, masked, np.stack(cases))
    pair_list = sorted({(int(d0), int(d1)) for d0, d1 in case_maps.reshape(-1, 2)})
    pair_idx = [[[pair_list.index((int(row[2 * p]), int(row[2 * p + 1]))) for p in range(win // 2)]
                 for row in case] for case in case_maps]
    return blocks, np.asarray(pair_list), pair_idx, q_rows_n * GRID_W, win * GRID_W


def _nbr_bias(rpb, pairs):
    depth, heads = rpb.shape[:2]
    c = np.arange(GRID_W)
    c0 = np.clip(c - NA_COLS // 2, 0, GRID_W - NA_COLS)
    col_mask = (c[None, :] >= c0[:, None]) & (c[None, :] < c0[:, None] + NA_COLS)
    col_idx = np.clip(c[None, :] - c[:, None] + NA_COLS - 1, 0, 2 * NA_COLS - 2)
    toe = jnp.where(col_mask, rpb.astype(F32)[:, :, :, col_idx], -jnp.inf)
    toe = jnp.concatenate([toe, jnp.full((depth, heads, 1, GRID_W, GRID_W), -jnp.inf, F32)], axis=2)
    return jnp.concatenate([jnp.take(toe, pairs[:, 0], axis=2), jnp.take(toe, pairs[:, 1], axis=2)], axis=-1)


def _attn_b(proj, pair_bias, layer, plan, cfg):
    b, t, _ = proj.shape
    heads = cfg["b_w"] // HEAD_DIM
    q_blk = cfg["off_bq"] // HEAD_DIM
    k_blk = q_blk + heads
    v_blk = k_blk + heads
    blocks, _, pair_idx, q_rows, k_rows = plan
    n_pair = pair_bias.shape[2]
    kern = functools.partial(_attn_b_kernel, n_ctx=cfg["n_ctx"], blocks=blocks, pair_idx=pair_idx,
                             q_rows=q_rows, k_rows=k_rows, scale=HEAD_DIM ** -0.5)
    return pl.pallas_call(
        kern,
        grid=(heads, b),
        in_specs=[
            pl.BlockSpec((1, t, HEAD_DIM), lambda hi, bi: (bi, 0, q_blk + hi)),
            pl.BlockSpec((1, t, HEAD_DIM), lambda hi, bi: (bi, 0, k_blk + hi)),
            pl.BlockSpec((1, t, HEAD_DIM), lambda hi, bi: (bi, 0, v_blk + hi)),
            pl.BlockSpec((None, 1, n_pair, GRID_W, 2 * GRID_W), lambda hi, bi: (layer, hi, 0, 0, 0)),
        ],
        out_specs=pl.BlockSpec((1, t, HEAD_DIM), lambda hi, bi: (bi, 0, hi)),
        out_shape=jax.ShapeDtypeStruct((b, t, cfg["b_w"]), BF16),
        scratch_shapes=[pltpu.VMEM((len(pair_idx), q_rows, k_rows), F32)],
        compiler_params=_params("arbitrary", "arbitrary"),
        name="attn_b",
    )(proj, proj, proj, pair_bias)


def _merge_kernel(g_ref, oa_ref, ob_ref, oc_ref, wga_ref, wgb_ref, wgc_ref, bga_ref, bgb_ref, bgc_ref,
                  wa_ref, wb_ref, wc_ref, o_ref):
    g_low = g_ref[...]
    acc = None
    for o_br, w_br, wg, bg in ((oa_ref, wa_ref, wga_ref, bga_ref), (ob_ref, wb_ref, wgb_ref, bgb_ref),
                               (oc_ref, wc_ref, wgc_ref, bgc_ref)):
        gate = jax.nn.sigmoid(_dot(g_low, wg[...].astype(BF16)) + bg[...])
        term = gate * _dot(o_br[...], w_br[...].astype(BF16))
        acc = term if acc is None else acc + term
    o_ref[...] = acc.astype(BF16)


def _merge(proj2, o_a, o_b, o_c, w_gate, b_gate, w_branch, layer, cfg):
    m = proj2.shape[0]
    d = w_branch.shape[2]
    rank = w_gate.shape[1]
    tm, tn = cfg["mm_tm"], cfg["mm_tn"]
    a_w, b_w, c_w = cfg["a_q_w"], cfg["b_w"], cfg["c_v_w"]
    g_blk = cfg["off_g"] // rank
    nb = d // tn
    gate_specs = [pl.BlockSpec((None, rank, tn), lambda i, j, br=br: (layer, 0, br * nb + j)) for br in range(3)]
    bias_specs = [pl.BlockSpec((None, 1, tn), lambda i, j, br=br: (layer, 0, br * nb + j)) for br in range(3)]
    return pl.pallas_call(
        _merge_kernel,
        grid=(m // tm, nb),
        in_specs=[
            pl.BlockSpec((tm, rank), lambda i, j: (i, g_blk)),
            pl.BlockSpec((tm, a_w), lambda i, j: (i, 0)),
            pl.BlockSpec((tm, b_w), lambda i, j: (i, 0)),
            pl.BlockSpec((tm, c_w), lambda i, j: (i, 0)),
            *gate_specs, *bias_specs,
            pl.BlockSpec((None, a_w, tn), lambda i, j: (layer, 0, j)),
            pl.BlockSpec((None, b_w, tn), lambda i, j: (layer, a_w // b_w, j)),
            pl.BlockSpec((None, c_w, tn), lambda i, j: (layer, (a_w + b_w) // c_w, j)),
        ],
        out_specs=pl.BlockSpec((tm, tn), lambda i, j: (i, j)),
        out_shape=jax.ShapeDtypeStruct((m, d), BF16),
        compiler_params=_params("arbitrary", "arbitrary"),
        name="merge",
    )(proj2, o_a, o_b, o_c, w_gate, w_gate, w_gate, b_gate, b_gate, b_gate, w_branch, w_branch, w_branch)


def _matmul_kernel(a_ref, w_ref, o_ref):
    w = w_ref[...]
    o_ref[...] = _dot(a_ref[...], w if w.dtype == BF16 else w.astype(BF16))


def _matmul_f32(a, w, layer, tm, tn, name, a_buffers=2):
    m, k = a.shape
    n = w.shape[2]
    return pl.pallas_call(
        _matmul_kernel,
        grid=(m // tm, n // tn),
        in_specs=[pl.BlockSpec((tm, k), lambda i, j: (i, 0), pipeline_mode=pl.Buffered(a_buffers)),
                  pl.BlockSpec((None, k, tn), lambda i, j: (layer, 0, j))],
        out_specs=pl.BlockSpec((tm, tn), lambda i, j: (i, j)),
        out_shape=jax.ShapeDtypeStruct((m, n), F32),
        compiler_params=_params("arbitrary", "arbitrary"),
        name=name,
    )(a, w)


def _ffn_in_kernel(h_ref, wg_ref, wu_ref, o_ref):
    h = h_ref[...]
    gate = _dot(h, wg_ref[...].astype(BF16))
    up = _dot(h, wu_ref[...].astype(BF16))
    o_ref[...] = (gate * jax.nn.sigmoid(gate) * up).astype(BF16)


def _ffn_in(h, w, layer, cfg):
    m, d = h.shape
    d_ff = w.shape[2] // 2
    tm, tn = cfg["big_tm"], cfg["ffn_tn"]
    up_blk = d_ff // tn
    return pl.pallas_call(
        _ffn_in_kernel,
        grid=(m // tm, d_ff // tn),
        in_specs=[pl.BlockSpec((tm, d), lambda i, j: (i, 0), pipeline_mode=pl.Buffered(1)),
                  pl.BlockSpec((None, d, tn), lambda i, j: (layer, 0, j)),
                  pl.BlockSpec((None, d, tn), lambda i, j: (layer, 0, up_blk + j))],
        out_specs=pl.BlockSpec((tm, tn), lambda i, j: (i, j)),
        out_shape=jax.ShapeDtypeStruct((m, d_ff), BF16),
        compiler_params=_params("arbitrary", "arbitrary"),
        name="ffn_in",
    )(h, w, w)


def _rope_tables(seq, n_ctx, batch, dim):
    t = np.arange(seq)
    rows = (t // GRID_W).astype(np.float32)
    cols = (t % GRID_W).astype(np.float32)
    n_pairs = dim // 4
    freqs = jnp.asarray(ROPE_THETA, F32) ** (-jnp.arange(n_pairs, dtype=F32) / n_pairs)
    ang = jnp.concatenate([jnp.asarray(rows)[:, None] * freqs, jnp.asarray(cols)[:, None] * freqs], axis=-1)
    sign = jnp.asarray(np.tile([-1.0, 1.0], dim // 2), F32)
    cos = jnp.tile(jnp.repeat(jnp.cos(ang), 2, axis=1), (1, HEAD_DIM // dim))
    sin = jnp.tile(jnp.repeat(jnp.sin(ang), 2, axis=1) * sign, (1, HEAD_DIM // dim))
    cos = jnp.concatenate([jnp.ones((n_ctx, HEAD_DIM), F32), cos], axis=0)
    sin = jnp.concatenate([jnp.zeros((n_ctx, HEAD_DIM), F32), sin], axis=0)
    return jnp.tile(cos, (batch, 1)), jnp.tile(sin, (batch, 1))


def _config(d, n_ctx, in_w, rank, d_ff):
    heads = d // HEAD_DIM
    a_q_w = (heads // 2) * HEAD_DIM
    a_kv_w = a_q_w // Q_GROUP
    b_w = (heads // 4) * HEAD_DIM
    c_qk_w = c_v_w = (heads // 4) * HEAD_DIM
    off_bq = a_q_w + 2 * a_kv_w
    off_cq = off_bq + 3 * b_w
    off_g = off_cq + 2 * c_qk_w + c_v_w
    assert off_g + rank == in_w and off_g % rank == 0
    assert a_q_w % b_w == 0 and (a_q_w + b_w) % c_v_w == 0
    in_tn = math.gcd(math.gcd(a_kv_w, b_w), 512)
    ffn_tn = math.gcd(d_ff, 512)
    return dict(n_ctx=n_ctx, a_q_w=a_q_w, a_kv_w=a_kv_w, b_w=b_w, c_qk_w=c_qk_w, c_v_w=c_v_w,
                off_bq=off_bq, off_cq=off_cq, off_g=off_g, in_tn=in_tn, ffn_tn=ffn_tn,
                mm_tn=min(512, d))


def kernel(x, c, ctx, c_ctx, ada_down, ada_up, ada_bias, norm_mix_pre, norm_mix_post, norm_ffn_pre,
           norm_ffn_post, w_in, a_q_norm, a_k_norm, b_rel_bias, c_lambda_q1, c_lambda_k1, c_lambda_q2,
           c_lambda_k2, c_subln, w_gate_up, b_gate, w_branch, w_out, w_ffn_in, w_ffn_out):
    batch, seq, d = x.shape
    n_ctx = ctx.shape[1]
    depth, _, in_w = w_in.shape
    rank = w_gate_up.shape[1]
    d_ff = w_ffn_out.shape[1]
    t = n_ctx + seq
    m = batch * t
    cfg = _config(d, n_ctx, in_w, rank, d_ff)
    assert n_ctx % ROW_TILE == 0 and seq % ROW_TILE == 0 and seq % (NA_ROWS * GRID_W) == 0
    cfg["mm_tm"] = math.gcd(m, MM_TILE_M)
    cfg["big_tm"] = max(m // k for k in range(1, m // ROW_ALIGN + 1)
                        if m % (k * ROW_ALIGN) == 0 and m // k <= MM_TILE_M_BIG)
    geom = (t // ROW_TILE, n_ctx // ROW_TILE, batch)

    w_ffn_out_b = w_ffn_out.astype(BF16)
    b_gate_r = b_gate.reshape(depth, 1, 3 * d)

    tabs = _rope_tables(seq, n_ctx, batch, HEAD_DIM) + _rope_tables(seq, n_ctx, batch, HEAD_DIM // 2)
    plan = _nbr_plan(seq)
    nbr_bias = _nbr_bias(b_rel_bias, plan[1])

    rows = -(-(batch + 1) // 8) * 8
    cond = jnp.zeros((rows, d), F32).at[:batch].set(c).at[batch].set(c_ctx)
    mods = _adaln(cond, ada_down, ada_up, ada_bias).reshape(depth, 6, rows, 1, d)

    xs, h = _embed(x, ctx, norm_mix_pre[0], mods, geom)
    for l in range(depth):
        lam_init = 0.8 - 0.6 * math.exp(-0.3 * l)
        proj = _inproj(h, w_in, l, a_q_norm[l], a_k_norm[l], tabs, cfg)
        proj3 = proj.reshape(batch, t, in_w)
        o_a = _attn_a(proj3, cfg).reshape(m, -1)
        o_b = _attn_b(proj3, nbr_bias, l, plan, cfg).reshape(m, -1)
        o_c = _attn_c(proj3, (c_lambda_q1[l], c_lambda_k1[l], c_lambda_q2[l], c_lambda_k2[l]),
                      c_subln[l], lam_init, cfg).reshape(m, -1)
        merged = _merge(proj, o_a, o_b, o_c, w_gate_up, b_gate_r, w_branch, l, cfg)
        mix = _matmul_f32(merged, w_out, l, cfg["big_tm"], cfg["mm_tn"], "out_proj", a_buffers=1)
        xs, h = _resid(xs, mix, norm_mix_post[l], norm_ffn_pre[l], mods, l, 2, l, 3, 4, geom)
        act = _ffn_in(h, w_ffn_in, l, cfg)
        f = _matmul_f32(act, w_ffn_out_b, l, min(512, cfg["mm_tm"]), cfg["mm_tn"], "ffn_out")
        if l + 1 < depth:
            xs, h = _resid(xs, f, norm_ffn_post[l], norm_mix_pre[l + 1], mods, l, 5, l + 1, 0, 1, geom)
    return _resid_out(xs, f, norm_ffn_post[depth - 1], mods, depth - 1, 5, geom).reshape(batch, seq, d)
```

```python
import functools
import math

import jax
import jax.numpy as jnp
import numpy as np
from jax import lax
from jax.experimental import pallas as pl
from jax.experimental.pallas import tpu as pltpu

HEAD_DIM = 128
GRID_W = 64
NA_ROWS = 8
NA_COLS = 16
Q_GROUP = 4
ROPE_THETA = 10000.0
EPS = 1e-6
LOG2_E = math.log2(math.e)
ROW_TILE = 256
MM_TILE_M = 1024
MM_TILE_M_BIG = 2304
SUB_TILES = 8
ROW_ALIGN = 128
VMEM_LIMIT = 56 * 1024 * 1024
BF16 = jnp.bfloat16
F32 = jnp.float32


def _params(*sem):
    return pltpu.CompilerParams(dimension_semantics=sem, vmem_limit_bytes=VMEM_LIMIT)


def _dot(a, b):
    return jnp.dot(a, b, preferred_element_type=F32)


def _dot_nt(a, b):
    return lax.dot_general(a, b, (((1,), (1,)), ((), ())), preferred_element_type=F32)


def _rms(x, gain):
    return x * lax.rsqrt(jnp.mean(x * x, axis=-1, keepdims=True) + EPS) * gain


def _adaln_kernel(cond_ref, down_ref, up_ref, bias_ref, o_ref):
    cond = cond_ref[...]
    act = (cond * jax.nn.sigmoid(cond)).astype(BF16)
    low = _dot(act, down_ref[0].astype(BF16))
    o_ref[0, 0] = _dot(low.astype(BF16), up_ref[0].astype(BF16)) + bias_ref[0]


def _adaln(cond, ada_down, ada_up, ada_bias):
    depth, d, rank = ada_down.shape
    rows = cond.shape[0]
    return pl.pallas_call(
        _adaln_kernel,
        grid=(depth, 6),
        in_specs=[
            pl.BlockSpec((rows, d), lambda l, k: (0, 0)),
            pl.BlockSpec((1, d, rank), lambda l, k: (l, 0, 0)),
            pl.BlockSpec((1, rank, d), lambda l, k: (l, 0, k)),
            pl.BlockSpec((1, 1, d), lambda l, k: (l, 0, k)),
        ],
        out_specs=pl.BlockSpec((1, 1, rows, d), lambda l, k: (l, k, 0, 0)),
        out_shape=jax.ShapeDtypeStruct((depth, 6, rows, d), F32),
        compiler_params=_params("arbitrary", "arbitrary"),
        name="adaln",
    )(cond, ada_down, ada_up, ada_bias.reshape(depth, 1, 6 * d))


def _mod_row(i, tiles_per_batch, ctx_tiles, batch):
    return jnp.where(i % tiles_per_batch < ctx_tiles, batch, i // tiles_per_batch)


def _embed_kernel(x_ref, ctx_ref, gain_ref, shift_ref, scale_ref, xs_ref, h_ref, *, tiles_per_batch, ctx_tiles):
    def emit(src):
        xs_ref[...] = src
        h_ref[...] = (_rms(src, gain_ref[...]) * (1.0 + scale_ref[0, 0, 0]) + shift_ref[0, 0, 0]).astype(BF16)

    is_ctx = pl.program_id(0) % tiles_per_batch < ctx_tiles

    @pl.when(is_ctx)
    def _():
        emit(ctx_ref[0])

    @pl.when(jnp.logical_not(is_ctx))
    def _():
        emit(x_ref[0])


def _embed(x, ctx, gain, mods, geom):
    batch, seq, d = x.shape
    tpb, ctx_tiles, _ = geom
    m = batch * tpb * ROW_TILE
    row = functools.partial(_mod_row, tiles_per_batch=tpb, ctx_tiles=ctx_tiles, batch=batch)
    tile = pl.BlockSpec((ROW_TILE, d), lambda i: (i, 0))
    kern = functools.partial(_embed_kernel, tiles_per_batch=tpb, ctx_tiles=ctx_tiles)
    return pl.pallas_call(
        kern,
        grid=(m // ROW_TILE,),
        in_specs=[
            pl.BlockSpec((1, ROW_TILE, d), lambda i: (i // tpb, jnp.maximum(i % tpb - ctx_tiles, 0), 0)),
            pl.BlockSpec((1, ROW_TILE, d), lambda i: (i // tpb, jnp.minimum(i % tpb, ctx_tiles - 1), 0)),
            pl.BlockSpec((1, d), lambda i: (0, 0)),
            pl.BlockSpec((1, 1, 1, 1, d), lambda i: (0, 0, row(i), 0, 0)),
            pl.BlockSpec((1, 1, 1, 1, d), lambda i: (0, 1, row(i), 0, 0)),
        ],
        out_specs=[tile, tile],
        out_shape=[jax.ShapeDtypeStruct((m, d), F32), jax.ShapeDtypeStruct((m, d), BF16)],
        compiler_params=_params("arbitrary"),
        name="embed",
    )(x, ctx, gain.reshape(1, d), mods, mods)


def _resid_kernel(x_ref, y_ref, post_ref, gate_ref, pre_ref, shift_ref, scale_ref, xo_ref, h_ref):
    x_new = x_ref[...] + gate_ref[0, 0, 0] * _rms(y_ref[...], post_ref[...])
    xo_ref[...] = x_new
    h = _rms(x_new, pre_ref[...])
    h_ref[...] = (h * (1.0 + scale_ref[0, 0, 0]) + shift_ref[0, 0, 0]).astype(BF16)


def _resid(xs, y, post_gain, pre_gain, mods, layer, k_gate, mod_layer, k_shift, k_scale, geom):
    m, d = xs.shape
    tpb, ctx_tiles, batch = geom
    row = functools.partial(_mod_row, tiles_per_batch=tpb, ctx_tiles=ctx_tiles, batch=batch)
    tile = pl.BlockSpec((ROW_TILE, d), lambda i: (i, 0))
    vec = pl.BlockSpec((1, d), lambda i: (0, 0))
    return pl.pallas_call(
        _resid_kernel,
        grid=(m // ROW_TILE,),
        in_specs=[
            tile, tile, vec,
            pl.BlockSpec((1, 1, 1, 1, d), lambda i: (layer, k_gate, row(i), 0, 0)),
            vec,
            pl.BlockSpec((1, 1, 1, 1, d), lambda i: (mod_layer, k_shift, row(i), 0, 0)),
            pl.BlockSpec((1, 1, 1, 1, d), lambda i: (mod_layer, k_scale, row(i), 0, 0)),
        ],
        out_specs=[tile, tile],
        out_shape=[jax.ShapeDtypeStruct((m, d), F32), jax.ShapeDtypeStruct((m, d), BF16)],
        compiler_params=_params("arbitrary"),
        name="resid",
    )(xs, y, post_gain.reshape(1, d), mods, pre_gain.reshape(1, d), mods, mods)


def _resid_out_kernel(x_ref, y_ref, post_ref, gate_ref, o_ref):
    o_ref[...] = x_ref[...] + gate_ref[0, 0, 0] * _rms(y_ref[...], post_ref[...])


def _resid_out(xs, y, post_gain, mods, layer, k_gate, geom):
    m, d = xs.shape
    tpb, ctx_tiles, batch = geom
    lat = tpb - ctx_tiles
    row = functools.partial(_mod_row, tiles_per_batch=tpb, ctx_tiles=ctx_tiles, batch=batch)
    tile = pl.BlockSpec((ROW_TILE, d), lambda i: (i, 0))
    return pl.pallas_call(
        _resid_out_kernel,
        grid=(m // ROW_TILE,),
        in_specs=[
            tile, tile, pl.BlockSpec((1, d), lambda i: (0, 0)),
            pl.BlockSpec((1, 1, 1, 1, d), lambda i: (layer, k_gate, row(i), 0, 0)),
        ],
        out_specs=pl.BlockSpec((ROW_TILE, d), lambda i: ((i // tpb) * lat + jnp.maximum(i % tpb - ctx_tiles, 0), 0)),
        out_shape=jax.ShapeDtypeStruct((batch * lat * ROW_TILE, d), F32),
        compiler_params=_params("arbitrary"),
        name="resid_out",
    )(xs, y, post_gain.reshape(1, d), mods)


def _rope(y, cos, sin):
    lane = lax.broadcasted_iota(jnp.int32, y.shape, 1)
    partner = jnp.where(lane % 2 == 0, pltpu.roll(y, HEAD_DIM - 1, 1), pltpu.roll(y, 1, 1))
    return y * cos + partner * sin


def _inproj_kernel(h_ref, w_ref, qg_ref, kg_ref, cosa_ref, sina_ref, cosc_ref, sinc_ref, o_ref,
                   *, tn, q_tiles, k_tiles, rope_c_ranges):
    j = pl.program_id(1)
    heads = tn // HEAD_DIM
    n_sub = max(k for k in range(1, SUB_TILES + 1) if h_ref.shape[0] % (16 * k) == 0)
    sub_rows = h_ref.shape[0] // n_sub

    def project(fn):
        for s in range(n_sub):
            rs = slice(s * sub_rows, (s + 1) * sub_rows)
            acc = _dot(h_ref[rs], w_ref[...].astype(BF16))
            if fn is None:
                o_ref[rs] = acc.astype(BF16)
                continue
            for hh in range(heads):
                sl = slice(hh * HEAD_DIM, (hh + 1) * HEAD_DIM)
                o_ref[rs, sl] = fn(acc[:, sl], rs).astype(BF16)

    is_q = j < q_tiles
    is_k = jnp.logical_and(j >= q_tiles, j < q_tiles + k_tiles)
    is_c = functools.reduce(jnp.logical_or,
                            [jnp.logical_and(j >= lo, j < hi) for lo, hi in rope_c_ranges])
    plain = jnp.logical_not(jnp.logical_or(jnp.logical_or(is_q, is_k), is_c))

    @pl.when(is_q)
    def _():
        project(lambda a, rs: _rope(_rms(a, qg_ref[...]), cosa_ref[rs], sina_ref[rs]))

    @pl.when(is_k)
    def _():
        project(lambda a, rs: _rope(_rms(a, kg_ref[...]), cosa_ref[rs], sina_ref[rs]))

    @pl.when(is_c)
    def _():
        project(lambda a, rs: _rope(a, cosc_ref[rs], sinc_ref[rs]))

    @pl.when(plain)
    def _():
        project(None)


def _inproj(h, w, layer, q_gain, k_gain, tabs, cfg):
    m, d = h.shape
    n = w.shape[2]
    tn, tm = cfg["in_tn"], cfg["big_tm"]
    tab = pl.BlockSpec((tm, HEAD_DIM), lambda i, j: (i, 0), pipeline_mode=pl.Buffered(1))
    vec = pl.BlockSpec((1, HEAD_DIM), lambda i, j: (0, 0))
    kern = functools.partial(
        _inproj_kernel, tn=tn, q_tiles=cfg["a_q_w"] // tn, k_tiles=cfg["a_kv_w"] // tn,
        rope_c_ranges=[(cfg["off_cq"] // tn, (cfg["off_cq"] + 2 * cfg["c_qk_w"]) // tn)])
    return pl.pallas_call(
        kern,
        grid=(m // tm, n // tn),
        in_specs=[
            pl.BlockSpec((tm, d), lambda i, j: (i, 0), pipeline_mode=pl.Buffered(1)),
            pl.BlockSpec((None, d, tn), lambda i, j: (layer, 0, j)),
            vec, vec, tab, tab, tab, tab,
        ],
        out_specs=pl.BlockSpec((tm, tn), lambda i, j: (i, j)),
        out_shape=jax.ShapeDtypeStruct((m, n), BF16),
        compiler_params=_params("arbitrary", "arbitrary"),
        name="inproj",
    )(h, w, q_gain.reshape(1, HEAD_DIM), k_gain.reshape(1, HEAD_DIM), *tabs)


def _softmax_pv(s, v):
    m = jnp.max(s, axis=-1, keepdims=True)
    p = jnp.exp(s - m)
    l = jnp.sum(p, axis=-1, keepdims=True)
    return _dot(p.astype(BF16), v) / l


ONES_ROWS = 16
EXP_CHUNK = 256


def _attend_t(q, k, vt, c):
    st = _dot_nt(k, q)
    m = jnp.max(st, axis=0, keepdims=True)
    acc = _dot(vt, jnp.exp2((st - m) * c).astype(BF16))
    d = vt.shape[0] - ONES_ROWS
    return acc[:d] / acc[d:d + 1]


def _qk_max(q, k, st_ref):
    st = _dot_nt(k, q)
    st_ref[...] = st
    return jnp.max(st, axis=0, keepdims=True)


def _exp_pv(st_ref, pt_ref, vt, m, c):
    n = st_ref.shape[0]
    for c0 in range(0, n, EXP_CHUNK):
        c1 = min(c0 + EXP_CHUNK, n)
        pt_ref[c0:c1] = jnp.exp2((st_ref[c0:c1] - m) * c).astype(BF16)
    acc = _dot(vt, pt_ref[...])
    d = vt.shape[0] - ONES_ROWS
    return acc[:d] / acc[d:d + 1]


def _attend_tiles(n_iter, n_items, load_q, store_o, k, vt, st_ref, pt_ref, m_ref, c):
    def scores(i, j):
        m_ref[j] = _qk_max(load_q(i, j), k, st_ref.at[j])

    for j in range(n_items - 1):
        scores(0, j)

    def body(i, carry):
        nxt = jnp.minimum(i + 1, n_iter - 1)
        scores(i, n_items - 1)
        for j in range(n_items):
            store_o(i, j, _exp_pv(st_ref.at[j], pt_ref.at[j % 2], vt, m_ref[j], c))
            if j < n_items - 1:
                scores(nxt, j)
        return carry

    lax.fori_loop(0, n_iter, body, 0)


def _fill_vt(vt_ref, v_ref):
    vt_ref[:HEAD_DIM] = v_ref[0].T
    vt_ref[HEAD_DIM:] = jnp.ones((ONES_ROWS, vt_ref.shape[1]), BF16)


def _attn_scratch(t, n_items):
    return [pltpu.VMEM((HEAD_DIM + ONES_ROWS, t), BF16), pltpu.VMEM((n_items, t, ROW_TILE), F32),
            pltpu.VMEM((2, t, ROW_TILE), BF16), pltpu.VMEM((n_items, 1, ROW_TILE), F32)]


def _attn_a_kernel(q_ref, k_ref, v_ref, o_ref, vt_ref, st_ref, pt_ref, m_ref, *, n_ctx, group, c):
    t = k_ref.shape[1]
    tq = ROW_TILE
    _fill_vt(vt_ref, v_ref)
    heads = [slice(g * HEAD_DIM, (g + 1) * HEAD_DIM) for g in range(group)]

    for r0 in range(0, n_ctx, tq):
        for sl in heads:
            o = _attend_t(q_ref[0, r0:r0 + tq, sl], k_ref[0, :n_ctx], vt_ref[:, :n_ctx], c)
            o_ref[0, r0:r0 + tq, sl] = o.T.astype(BF16)

    def rows(i):
        return pl.ds(pl.multiple_of(n_ctx + i * tq, tq), tq)

    def load_q(i, g):
        return q_ref[0, rows(i), heads[g]]

    def store_o(i, g, o):
        o_ref[0, rows(i), heads[g]] = o.T.astype(BF16)

    _attend_tiles((t - n_ctx) // tq, group, load_q, store_o, k_ref[0], vt_ref[...], st_ref, pt_ref, m_ref, c)


def _attn_a(proj, cfg):
    b, t, _ = proj.shape
    gw = Q_GROUP * HEAD_DIM
    kvh = cfg["a_kv_w"] // HEAD_DIM
    k_blk = cfg["a_q_w"] // HEAD_DIM
    v_blk = (cfg["a_q_w"] + cfg["a_kv_w"]) // HEAD_DIM
    kern = functools.partial(_attn_a_kernel, n_ctx=cfg["n_ctx"], group=Q_GROUP, c=HEAD_DIM ** -0.5 * LOG2_E)
    return pl.pallas_call(
        kern,
        grid=(b, kvh),
        in_specs=[
            pl.BlockSpec((1, t, gw), lambda bi, hi: (bi, 0, hi)),
            pl.BlockSpec((1, t, HEAD_DIM), lambda bi, hi: (bi, 0, k_blk + hi)),
            pl.BlockSpec((1, t, HEAD_DIM), lambda bi, hi: (bi, 0, v_blk + hi)),
        ],
        out_specs=pl.BlockSpec((1, t, gw), lambda bi, hi: (bi, 0, hi)),
        out_shape=jax.ShapeDtypeStruct((b, t, cfg["a_q_w"]), BF16),
        scratch_shapes=_attn_scratch(t, Q_GROUP),
        compiler_params=_params("arbitrary", "arbitrary"),
        name="attn_a",
    )(proj, proj, proj)


def _attn_c_kernel(q_ref, k_ref, v_ref, l1_ref, l2_ref, l3_ref, l4_ref, gain_ref, o_ref,
                   vt_ref, st_ref, pt_ref, m_ref, *, n_ctx, c, lam_init):
    t = k_ref.shape[1]
    tq = ROW_TILE
    _fill_vt(vt_ref, v_ref)
    lam = (jnp.exp(jnp.sum(l1_ref[...] * l2_ref[...], axis=-1, keepdims=True))
           - jnp.exp(jnp.sum(l3_ref[...] * l4_ref[...], axis=-1, keepdims=True)) + lam_init)
    lane = lax.broadcasted_iota(jnp.int32, (tq, HEAD_DIM), 1)
    first = lane < HEAD_DIM // 2

    def component(q, comp):
        return jnp.where(first == (comp == 0), q, jnp.zeros_like(q))

    def finish(o1, o2):
        o = (o1 - lam * o2).T
        return (_rms(o, gain_ref[...]) * (1.0 - lam_init)).astype(BF16)

    for r0 in range(0, n_ctx, tq):
        q = q_ref[0, r0:r0 + tq]
        o1, o2 = [_attend_t(component(q, comp), k_ref[0, :n_ctx], vt_ref[:, :n_ctx], c) for comp in range(2)]
        o_ref[0, r0:r0 + tq] = finish(o1, o2)

    def rows(i, j):
        return pl.ds(pl.multiple_of(n_ctx + (2 * i + j // 2) * tq, tq), tq)

    def load_q(i, j):
        return component(q_ref[0, rows(i, j)], j % 2)

    pending = {}

    def store_o(i, j, o):
        if j % 2 == 0:
            pending[0] = o
        else:
            o_ref[0, rows(i, j)] = finish(pending.pop(0), o)

    _attend_tiles((t - n_ctx) // (2 * tq), 4, load_q, store_o, k_ref[0], vt_ref[...], st_ref, pt_ref, m_ref, c)


def _attn_c(proj, lams, subln, lam_init, cfg):
    b, t, _ = proj.shape
    heads = cfg["c_v_w"] // HEAD_DIM
    q_blk = cfg["off_cq"] // HEAD_DIM
    k_blk = q_blk + cfg["c_qk_w"] // HEAD_DIM
    v_blk = k_blk + cfg["c_qk_w"] // HEAD_DIM
    qk_dim = HEAD_DIM // 2
    lam_spec = pl.BlockSpec((1, qk_dim), lambda bi, hi: (0, 0))
    kern = functools.partial(_attn_c_kernel, n_ctx=cfg["n_ctx"], c=qk_dim ** -0.5 * LOG2_E, lam_init=lam_init)
    return pl.pallas_call(
        kern,
        grid=(b, heads),
        in_specs=[
            pl.BlockSpec((1, t, HEAD_DIM), lambda bi, hi: (bi, 0, q_blk + hi)),
            pl.BlockSpec((1, t, HEAD_DIM), lambda bi, hi: (bi, 0, k_blk + hi)),
            pl.BlockSpec((1, t, HEAD_DIM), lambda bi, hi: (bi, 0, v_blk + hi)),
            lam_spec, lam_spec, lam_spec, lam_spec,
            pl.BlockSpec((1, HEAD_DIM), lambda bi, hi: (0, 0)),
        ],
        out_specs=pl.BlockSpec((1, t, HEAD_DIM), lambda bi, hi: (bi, 0, hi)),
        out_shape=jax.ShapeDtypeStruct((b, t, cfg["c_v_w"]), BF16),
        scratch_shapes=_attn_scratch(t, 4),
        compiler_params=_params("arbitrary", "arbitrary"),
        name="attn_c",
    )(proj, proj, proj, *[v.reshape(1, qk_dim) for v in lams], subln.reshape(1, HEAD_DIM))


def _attn_b_kernel(q_ref, k_ref, v_ref, pair_ref, o_ref, bias_ref, *, n_ctx, blocks, pair_idx, q_rows, k_rows,
                   scale):
    @pl.when(pl.program_id(1) == 0)
    def _():
        for case, case_idx in enumerate(pair_idx):
            for a, row_idx in enumerate(case_idx):
                for p, pair in enumerate(row_idx):
                    bias_ref[case, a * GRID_W:(a + 1) * GRID_W, 2 * p * GRID_W:2 * (p + 1) * GRID_W] = (
                        pair_ref[0, pair])

    k_ctx = k_ref[0, :n_ctx]
    v_ctx = v_ref[0, :n_ctx]
    o_ref[0, :n_ctx] = _softmax_pv(_dot_nt(q_ref[0, :n_ctx], k_ctx) * scale, v_ctx).astype(BF16)
    for q0, k0, case in blocks:
        q = q_ref[0, n_ctx + q0:n_ctx + q0 + q_rows]
        k_win = k_ref[0, n_ctx + k0:n_ctx + k0 + k_rows]
        v_win = v_ref[0, n_ctx + k0:n_ctx + k0 + k_rows]
        s_loc = _dot_nt(q, k_win) * scale + bias_ref[case]
        s_ctx = _dot_nt(q, k_ctx) * scale
        m = jnp.maximum(jnp.max(s_loc, axis=-1, keepdims=True), jnp.max(s_ctx, axis=-1, keepdims=True))
        p_loc = jnp.exp(s_loc - m)
        p_ctx = jnp.exp(s_ctx - m)
        l = jnp.sum(p_loc, axis=-1, keepdims=True) + jnp.sum(p_ctx, axis=-1, keepdims=True)
        o = (_dot(p_loc.astype(BF16), v_win) + _dot(p_ctx.astype(BF16), v_ctx)) / l
        o_ref[0, n_ctx + q0:n_ctx + q0 + q_rows] = o.astype(BF16)


def _nbr_plan(seq):
    rows = seq // GRID_W
    kr = min(NA_ROWS, rows)
    q_rows_n = min(NA_ROWS, rows)
    win = min(2 * NA_ROWS, rows)
    cases, blocks = [], []
    for i in range(rows // q_rows_n):
        ws = int(np.clip(i * q_rows_n - kr // 2, 0, rows - win))
        drmap = np.full((q_rows_n, win), -1, np.int64)
        for a in range(q_rows_n):
            r = i * q_rows_n + a
            r0 = int(np.clip(r - kr // 2, 0, rows - kr))
            for w in range(win):
                kr_abs = ws + w
                if r0 <= kr_abs < r0 + kr:
                    drmap[a, w] = kr_abs - r + (NA_ROWS - 1)
        key = drmap.tobytes()
        keys = [c.tobytes() for c in cases]
        if key not in keys:
            cases.append(drmap)
            keys.append(key)
        blocks.append((i * q_rows_n * GRID_W, ws * GRID_W, keys.index(key)))
    masked = 2 * NA_ROWS - 1
    case_maps = np.where(np.stack(cases) < 0, masked, np.stack(cases))
    pair_list = sorted({(int(d0), int(d1)) for d0, d1 in case_maps.reshape(-1, 2)})
    pair_idx = [[[pair_list.index((int(row[2 * p]), int(row[2 * p + 1]))) for p in range(win // 2)]
                 for row in case] for case in case_maps]
    return blocks, np.asarray(pair_list), pair_idx, q_rows_n * GRID_W, win * GRID_W


def _nbr_bias(rpb, pairs):
    depth, heads = rpb.shape[:2]
    c = np.arange(GRID_W)
    c0 = np.clip(c - NA_COLS // 2, 0, GRID_W - NA_COLS)
    col_mask = (c[None, :] >= c0[:, None]) & (c[None, :] < c0[:, None] + NA_COLS)
    col_idx = np.clip(c[None, :] - c[:, None] + NA_COLS - 1, 0, 2 * NA_COLS - 2)
    toe = jnp.where(col_mask, rpb.astype(F32)[:, :, :, col_idx], -jnp.inf)
    toe = jnp.concatenate([toe, jnp.full((depth, heads, 1, GRID_W, GRID_W), -jnp.inf, F32)], axis=2)
    return jnp.concatenate([jnp.take(toe, pairs[:, 0], axis=2), jnp.take(toe, pairs[:, 1], axis=2)], axis=-1)


def _attn_b(proj, pair_bias, layer, plan, cfg):
    b, t, _ = proj.shape
    heads = cfg["b_w"] // HEAD_DIM
    q_blk = cfg["off_bq"] // HEAD_DIM
    k_blk = q_blk + heads
    v_blk = k_blk + heads
    blocks, _, pair_idx, q_rows, k_rows = plan
    n_pair = pair_bias.shape[2]
    kern = functools.partial(_attn_b_kernel, n_ctx=cfg["n_ctx"], blocks=blocks, pair_idx=pair_idx,
                             q_rows=q_rows, k_rows=k_rows, scale=HEAD_DIM ** -0.5)
    return pl.pallas_call(
        kern,
        grid=(heads, b),
        in_specs=[
            pl.BlockSpec((1, t, HEAD_DIM), lambda hi, bi: (bi, 0, q_blk + hi)),
            pl.BlockSpec((1, t, HEAD_DIM), lambda hi, bi: (bi, 0, k_blk + hi)),
            pl.BlockSpec((1, t, HEAD_DIM), lambda hi, bi: (bi, 0, v_blk + hi)),
            pl.BlockSpec((None, 1, n_pair, GRID_W, 2 * GRID_W), lambda hi, bi: (layer, hi, 0, 0, 0)),
        ],
        out_specs=pl.BlockSpec((1, t, HEAD_DIM), lambda hi, bi: (bi, 0, hi)),
        out_shape=jax.ShapeDtypeStruct((b, t, cfg["b_w"]), BF16),
        scratch_shapes=[pltpu.VMEM((len(pair_idx), q_rows, k_rows), F32)],
        compiler_params=_params("arbitrary", "arbitrary"),
        name="attn_b",
    )(proj, proj, proj, pair_bias)


def _merge_kernel(g_ref, oa_ref, ob_ref, oc_ref, wga_ref, wgb_ref, wgc_ref, bga_ref, bgb_ref, bgc_ref,
                  wa_ref, wb_ref, wc_ref, o_ref):
    g_low = g_ref[...]
    acc = None
    for o_br, w_br, wg, bg in ((oa_ref, wa_ref, wga_ref, bga_ref), (ob_ref, wb_ref, wgb_ref, bgb_ref),
                               (oc_ref, wc_ref, wgc_ref, bgc_ref)):
        gate = jax.nn.sigmoid(_dot(g_low, wg[...].astype(BF16)) + bg[...])
        term = gate * _dot(o_br[...], w_br[...].astype(BF16))
        acc = term if acc is None else acc + term
    o_ref[...] = acc.astype(BF16)


def _merge(proj2, o_a, o_b, o_c, w_gate, b_gate, w_branch, layer, cfg):
    m = proj2.shape[0]
    d = w_branch.shape[2]
    rank = w_gate.shape[1]
    tm, tn = cfg["mm_tm"], cfg["mm_tn"]
    a_w, b_w, c_w = cfg["a_q_w"], cfg["b_w"], cfg["c_v_w"]
    g_blk = cfg["off_g"] // rank
    nb = d // tn
    gate_specs = [pl.BlockSpec((None, rank, tn), lambda i, j, br=br: (layer, 0, br * nb + j)) for br in range(3)]
    bias_specs = [pl.BlockSpec((None, 1, tn), lambda i, j, br=br: (layer, 0, br * nb + j)) for br in range(3)]
    return pl.pallas_call(
        _merge_kernel,
        grid=(m // tm, nb),
        in_specs=[
            pl.BlockSpec((tm, rank), lambda i, j: (i, g_blk)),
            pl.BlockSpec((tm, a_w), lambda i, j: (i, 0)),
            pl.BlockSpec((tm, b_w), lambda i, j: (i, 0)),
            pl.BlockSpec((tm, c_w), lambda i, j: (i, 0)),
            *gate_specs, *bias_specs,
            pl.BlockSpec((None, a_w, tn), lambda i, j: (layer, 0, j)),
            pl.BlockSpec((None, b_w, tn), lambda i, j: (layer, a_w // b_w, j)),
            pl.BlockSpec((None, c_w, tn), lambda i, j: (layer, (a_w + b_w) // c_w, j)),
        ],
        out_specs=pl.BlockSpec((tm, tn), lambda i, j: (i, j)),
        out_shape=jax.ShapeDtypeStruct((m, d), BF16),
        compiler_params=_params("arbitrary", "arbitrary"),
        name="merge",
    )(proj2, o_a, o_b, o_c, w_gate, w_gate, w_gate, b_gate, b_gate, b_gate, w_branch, w_branch, w_branch)


def _matmul_kernel(a_ref, w_ref, o_ref):
    o_ref[...] = _dot(a_ref[...], w_ref[...].astype(BF16))


def _matmul_f32(a, w, layer, tm, tn, name, a_buffers=2):
    m, k = a.shape
    n = w.shape[2]
    return pl.pallas_call(
        _matmul_kernel,
        grid=(m // tm, n // tn),
        in_specs=[pl.BlockSpec((tm, k), lambda i, j: (i, 0), pipeline_mode=pl.Buffered(a_buffers)),
                  pl.BlockSpec((None, k, tn), lambda i, j: (layer, 0, j))],
        out_specs=pl.BlockSpec((tm, tn), lambda i, j: (i, j)),
        out_shape=jax.ShapeDtypeStruct((m, n), F32),
        compiler_params=_params("arbitrary", "arbitrary"),
        name=name,
    )(a, w)


def _ffn_in_kernel(h_ref, wg_ref, wu_ref, o_ref):
    h = h_ref[...]
    gate = _dot(h, wg_ref[...].astype(BF16))
    up = _dot(h, wu_ref[...].astype(BF16))
    o_ref[...] = (gate * jax.nn.sigmoid(gate) * up).astype(BF16)


def _ffn_in(h, w, layer, cfg):
    m, d = h.shape
    d_ff = w.shape[2] // 2
    tm, tn = cfg["big_tm"], cfg["ffn_tn"]
    up_blk = d_ff // tn
    return pl.pallas_call(
        _ffn_in_kernel,
        grid=(m // tm, d_ff // tn),
        in_specs=[pl.BlockSpec((tm, d), lambda i, j: (i, 0), pipeline_mode=pl.Buffered(1)),
                  pl.BlockSpec((None, d, tn), lambda i, j: (layer, 0, j)),
                  pl.BlockSpec((None, d, tn), lambda i, j: (layer, 0, up_blk + j))],
        out_specs=pl.BlockSpec((tm, tn), lambda i, j: (i, j)),
        out_shape=jax.ShapeDtypeStruct((m, d_ff), BF16),
        compiler_params=_params("arbitrary", "arbitrary"),
        name="ffn_in",
    )(h, w, w)


def _rope_tables(seq, n_ctx, batch, dim):
    t = np.arange(seq)
    rows = (t // GRID_W).astype(np.float32)
    cols = (t % GRID_W).astype(np.float32)
    n_pairs = dim // 4
    freqs = jnp.asarray(ROPE_THETA, F32) ** (-jnp.arange(n_pairs, dtype=F32) / n_pairs)
    ang = jnp.concatenate([jnp.asarray(rows)[:, None] * freqs, jnp.asarray(cols)[:, None] * freqs], axis=-1)
    sign = jnp.asarray(np.tile([-1.0, 1.0], dim // 2), F32)
    cos = jnp.tile(jnp.repeat(jnp.cos(ang), 2, axis=1), (1, HEAD_DIM // dim))
    sin = jnp.tile(jnp.repeat(jnp.sin(ang), 2, axis=1) * sign, (1, HEAD_DIM // dim))
    cos = jnp.concatenate([jnp.ones((n_ctx, HEAD_DIM), F32), cos], axis=0)
    sin = jnp.concatenate([jnp.zeros((n_ctx, HEAD_DIM), F32), sin], axis=0)
    return jnp.tile(cos, (batch, 1)), jnp.tile(sin, (batch, 1))


def _config(d, n_ctx, in_w, rank, d_ff):
    heads = d // HEAD_DIM
    a_q_w = (heads // 2) * HEAD_DIM
    a_kv_w = a_q_w // Q_GROUP
    b_w = (heads // 4) * HEAD_DIM
    c_qk_w = c_v_w = (heads // 4) * HEAD_DIM
    off_bq = a_q_w + 2 * a_kv_w
    off_cq = off_bq + 3 * b_w
    off_g = off_cq + 2 * c_qk_w + c_v_w
    assert off_g + rank == in_w and off_g % rank == 0
    assert a_q_w % b_w == 0 and (a_q_w + b_w) % c_v_w == 0
    in_tn = math.gcd(math.gcd(a_kv_w, b_w), 512)
    ffn_tn = math.gcd(d_ff, 512)
    return dict(n_ctx=n_ctx, a_q_w=a_q_w, a_kv_w=a_kv_w, b_w=b_w, c_qk_w=c_qk_w, c_v_w=c_v_w,
                off_bq=off_bq, off_cq=off_cq, off_g=off_g, in_tn=in_tn, ffn_tn=ffn_tn,
                mm_tn=min(512, d))


def kernel(x, c, ctx, c_ctx, ada_down, ada_up, ada_bias, norm_mix_pre, norm_mix_post, norm_ffn_pre,
           norm_ffn_post, w_in, a_q_norm, a_k_norm, b_rel_bias, c_lambda_q1, c_lambda_k1, c_lambda_q2,
           c_lambda_k2, c_subln, w_gate_up, b_gate, w_branch, w_out, w_ffn_in, w_ffn_out):
    batch, seq, d = x.shape
    n_ctx = ctx.shape[1]
    depth, _, in_w = w_in.shape
    rank = w_gate_up.shape[1]
    d_ff = w_ffn_out.shape[1]
    t = n_ctx + seq
    m = batch * t
    cfg = _config(d, n_ctx, in_w, rank, d_ff)
    assert n_ctx % ROW_TILE == 0 and seq % ROW_TILE == 0 and seq % (NA_ROWS * GRID_W) == 0
    cfg["mm_tm"] = math.gcd(m, MM_TILE_M)
    cfg["big_tm"] = max(m // k for k in range(1, m // ROW_ALIGN + 1)
                        if m % (k * ROW_ALIGN) == 0 and m // k <= MM_TILE_M_BIG)
    cfg["ffn_out_tm"] = max(m // k for k in range(1, m // 16 + 1)
                            if m % (k * 16) == 0 and m // k <= MM_TILE_M_BIG // 2)
    geom = (t // ROW_TILE, n_ctx // ROW_TILE, batch)

    b_gate_r = b_gate.reshape(depth, 1, 3 * d)

    tabs = _rope_tables(seq, n_ctx, batch, HEAD_DIM) + _rope_tables(seq, n_ctx, batch, HEAD_DIM // 2)
    plan = _nbr_plan(seq)
    nbr_bias = _nbr_bias(b_rel_bias, plan[1])

    rows = -(-(batch + 1) // 8) * 8
    cond = jnp.zeros((rows, d), F32).at[:batch].set(c).at[batch].set(c_ctx)
    mods = _adaln(cond, ada_down, ada_up, ada_bias).reshape(depth, 6, rows, 1, d)

    xs, h = _embed(x, ctx, norm_mix_pre[0], mods, geom)
    for l in range(depth):
        lam_init = 0.8 - 0.6 * math.exp(-0.3 * l)
        proj = _inproj(h, w_in, l, a_q_norm[l], a_k_norm[l], tabs, cfg)
        proj3 = proj.reshape(batch, t, in_w)
        o_a = _attn_a(proj3, cfg).reshape(m, -1)
        o_b = _attn_b(proj3, nbr_bias, l, plan, cfg).reshape(m, -1)
        o_c = _attn_c(proj3, (c_lambda_q1[l], c_lambda_k1[l], c_lambda_q2[l], c_lambda_k2[l]),
                      c_subln[l], lam_init, cfg).reshape(m, -1)
        merged = _merge(proj, o_a, o_b, o_c, w_gate_up, b_gate_r, w_branch, l, cfg)
        mix = _matmul_f32(merged, w_out, l, cfg["big_tm"], cfg["mm_tn"], "out_proj", a_buffers=1)
        xs, h = _resid(xs, mix, norm_mix_post[l], norm_ffn_pre[l], mods, l, 2, l, 3, 4, geom)
        act = _ffn_in(h, w_ffn_in, l, cfg)
        f = _matmul_f32(act, w_ffn_out, l, cfg["ffn_out_tm"], min(256, d), "ffn_out", a_buffers=1)
        if l + 1 < depth:
            xs, h = _resid(xs, f, norm_ffn_post[l], norm_mix_pre[l + 1], mods, l, 5, l + 1, 0, 1, geom)
    return _resid_out(xs, f, norm_ffn_post[depth - 1], mods, depth - 1, 5, geom).reshape(batch, seq, d)
```

```python
import functools
import math

import jax
import jax.numpy as jnp
import numpy as np
from jax import lax
from jax.experimental import pallas as pl
from jax.experimental.pallas import tpu as pltpu

HEAD_DIM = 128
GRID_W = 64
NA_ROWS = 8
NA_COLS = 16
Q_GROUP = 4
ROPE_THETA = 10000.0
EPS = 1e-6
LOG2_E = math.log2(math.e)
ROW_TILE = 256
MM_TILE_M = 1024
MM_TILE_M_BIG = 2304
SUB_TILES = 8
ROW_ALIGN = 128
VMEM_LIMIT = 56 * 1024 * 1024
BF16 = jnp.bfloat16
F32 = jnp.float32


def _params(*sem):
    return pltpu.CompilerParams(dimension_semantics=sem, vmem_limit_bytes=VMEM_LIMIT)


def _dot(a, b):
    return jnp.dot(a, b, preferred_element_type=F32)


def _dot_nt(a, b):
    return lax.dot_general(a, b, (((1,), (1,)), ((), ())), preferred_element_type=F32)


def _rms(x, gain):
    return x * lax.rsqrt(jnp.mean(x * x, axis=-1, keepdims=True) + EPS) * gain


def _adaln_kernel(cond_ref, down_ref, up_ref, bias_ref, o_ref):
    cond = cond_ref[...]
    act = (cond * jax.nn.sigmoid(cond)).astype(BF16)
    low = _dot(act, down_ref[0].astype(BF16))
    o_ref[0, 0] = _dot(low.astype(BF16), up_ref[0].astype(BF16)) + bias_ref[0]


def _adaln(cond, ada_down, ada_up, ada_bias):
    depth, d, rank = ada_down.shape
    rows = cond.shape[0]
    return pl.pallas_call(
        _adaln_kernel,
        grid=(depth, 6),
        in_specs=[
            pl.BlockSpec((rows, d), lambda l, k: (0, 0)),
            pl.BlockSpec((1, d, rank), lambda l, k: (l, 0, 0)),
            pl.BlockSpec((1, rank, d), lambda l, k: (l, 0, k)),
            pl.BlockSpec((1, 1, d), lambda l, k: (l, 0, k)),
        ],
        out_specs=pl.BlockSpec((1, 1, rows, d), lambda l, k: (l, k, 0, 0)),
        out_shape=jax.ShapeDtypeStruct((depth, 6, rows, d), F32),
        compiler_params=_params("arbitrary", "arbitrary"),
        name="adaln",
    )(cond, ada_down, ada_up, ada_bias.reshape(depth, 1, 6 * d))


def _mod_row(i, tiles_per_batch, ctx_tiles, batch):
    return jnp.where(i % tiles_per_batch < ctx_tiles, batch, i // tiles_per_batch)


def _embed_kernel(x_ref, ctx_ref, gain_ref, shift_ref, scale_ref, xs_ref, h_ref, *, tiles_per_batch, ctx_tiles):
    def emit(src):
        xs_ref[...] = src
        h_ref[...] = (_rms(src, gain_ref[...]) * (1.0 + scale_ref[0, 0, 0]) + shift_ref[0, 0, 0]).astype(BF16)

    is_ctx = pl.program_id(0) % tiles_per_batch < ctx_tiles

    @pl.when(is_ctx)
    def _():
        emit(ctx_ref[0])

    @pl.when(jnp.logical_not(is_ctx))
    def _():
        emit(x_ref[0])


def _embed(x, ctx, gain, mods, geom):
    batch, seq, d = x.shape
    tpb, ctx_tiles, _ = geom
    m = batch * tpb * ROW_TILE
    row = functools.partial(_mod_row, tiles_per_batch=tpb, ctx_tiles=ctx_tiles, batch=batch)
    tile = pl.BlockSpec((ROW_TILE, d), lambda i: (i, 0))
    kern = functools.partial(_embed_kernel, tiles_per_batch=tpb, ctx_tiles=ctx_tiles)
    return pl.pallas_call(
        kern,
        grid=(m // ROW_TILE,),
        in_specs=[
            pl.BlockSpec((1, ROW_TILE, d), lambda i: (i // tpb, jnp.maximum(i % tpb - ctx_tiles, 0), 0)),
            pl.BlockSpec((1, ROW_TILE, d), lambda i: (i // tpb, jnp.minimum(i % tpb, ctx_tiles - 1), 0)),
            pl.BlockSpec((1, d), lambda i: (0, 0)),
            pl.BlockSpec((1, 1, 1, 1, d), lambda i: (0, 0, row(i), 0, 0)),
            pl.BlockSpec((1, 1, 1, 1, d), lambda i: (0, 1, row(i), 0, 0)),
        ],
        out_specs=[tile, tile],
        out_shape=[jax.ShapeDtypeStruct((m, d), F32), jax.ShapeDtypeStruct((m, d), BF16)],
        compiler_params=_params("arbitrary"),
        name="embed",
    )(x, ctx, gain.reshape(1, d), mods, mods)


def _resid_kernel(x_ref, y_ref, post_ref, gate_ref, pre_ref, shift_ref, scale_ref, xo_ref, h_ref):
    x_new = x_ref[...] + gate_ref[0, 0, 0] * _rms(y_ref[...], post_ref[...])
    xo_ref[...] = x_new
    h = _rms(x_new, pre_ref[...])
    h_ref[...] = (h * (1.0 + scale_ref[0, 0, 0]) + shift_ref[0, 0, 0]).astype(BF16)


def _resid(xs, y, post_gain, pre_gain, mods, layer, k_gate, mod_layer, k_shift, k_scale, geom):
    m, d = xs.shape
    tpb, ctx_tiles, batch = geom
    row = functools.partial(_mod_row, tiles_per_batch=tpb, ctx_tiles=ctx_tiles, batch=batch)
    tile = pl.BlockSpec((ROW_TILE, d), lambda i: (i, 0))
    vec = pl.BlockSpec((1, d), lambda i: (0, 0))
    return pl.pallas_call(
        _resid_kernel,
        grid=(m // ROW_TILE,),
        in_specs=[
            tile, tile, vec,
            pl.BlockSpec((1, 1, 1, 1, d), lambda i: (layer, k_gate, row(i), 0, 0)),
            vec,
            pl.BlockSpec((1, 1, 1, 1, d), lambda i: (mod_layer, k_shift, row(i), 0, 0)),
            pl.BlockSpec((1, 1, 1, 1, d), lambda i: (mod_layer, k_scale, row(i), 0, 0)),
        ],
        out_specs=[tile, tile],
        out_shape=[jax.ShapeDtypeStruct((m, d), F32), jax.ShapeDtypeStruct((m, d), BF16)],
        compiler_params=_params("arbitrary"),
        name="resid",
    )(xs, y, post_gain.reshape(1, d), mods, pre_gain.reshape(1, d), mods, mods)


def _resid_out_kernel(x_ref, y_ref, post_ref, gate_ref, o_ref):
    o_ref[...] = x_ref[...] + gate_ref[0, 0, 0] * _rms(y_ref[...], post_ref[...])


def _resid_out(xs, y, post_gain, mods, layer, k_gate, geom):
    m, d = xs.shape
    tpb, ctx_tiles, batch = geom
    lat = tpb - ctx_tiles
    row = functools.partial(_mod_row, tiles_per_batch=tpb, ctx_tiles=ctx_tiles, batch=batch)
    tile = pl.BlockSpec((ROW_TILE, d), lambda i: (i, 0))
    return pl.pallas_call(
        _resid_out_kernel,
        grid=(m // ROW_TILE,),
        in_specs=[
            tile, tile, pl.BlockSpec((1, d), lambda i: (0, 0)),
            pl.BlockSpec((1, 1, 1, 1, d), lambda i: (layer, k_gate, row(i), 0, 0)),
        ],
        out_specs=pl.BlockSpec((ROW_TILE, d), lambda i: ((i // tpb) * lat + jnp.maximum(i % tpb - ctx_tiles, 0), 0)),
        out_shape=jax.ShapeDtypeStruct((batch * lat * ROW_TILE, d), F32),
        compiler_params=_params("arbitrary"),
        name="resid_out",
    )(xs, y, post_gain.reshape(1, d), mods)


def _rope(y, cos, sin):
    lane = lax.broadcasted_iota(jnp.int32, y.shape, 1)
    partner = jnp.where(lane % 2 == 0, pltpu.roll(y, HEAD_DIM - 1, 1), pltpu.roll(y, 1, 1))
    return y * cos + partner * sin


def _inproj_kernel(h_ref, w_ref, qg_ref, kg_ref, cosa_ref, sina_ref, cosc_ref, sinc_ref, o_ref,
                   *, tn, q_tiles, k_tiles, rope_c_ranges):
    j = pl.program_id(1)
    heads = tn // HEAD_DIM
    n_sub = max(k for k in range(1, SUB_TILES + 1) if h_ref.shape[0] % (16 * k) == 0)
    sub_rows = h_ref.shape[0] // n_sub

    def project(fn):
        for s in range(n_sub):
            rs = slice(s * sub_rows, (s + 1) * sub_rows)
            acc = _dot(h_ref[rs], w_ref[...].astype(BF16))
            if fn is None:
                o_ref[rs] = acc.astype(BF16)
                continue
            for hh in range(heads):
                sl = slice(hh * HEAD_DIM, (hh + 1) * HEAD_DIM)
                o_ref[rs, sl] = fn(acc[:, sl], rs).astype(BF16)

    is_q = j < q_tiles
    is_k = jnp.logical_and(j >= q_tiles, j < q_tiles + k_tiles)
    is_c = functools.reduce(jnp.logical_or,
                            [jnp.logical_and(j >= lo, j < hi) for lo, hi in rope_c_ranges])
    plain = jnp.logical_not(jnp.logical_or(jnp.logical_or(is_q, is_k), is_c))

    @pl.when(is_q)
    def _():
        project(lambda a, rs: _rope(_rms(a, qg_ref[...]), cosa_ref[rs], sina_ref[rs]))

    @pl.when(is_k)
    def _():
        project(lambda a, rs: _rope(_rms(a, kg_ref[...]), cosa_ref[rs], sina_ref[rs]))

    @pl.when(is_c)
    def _():
        project(lambda a, rs: _rope(a, cosc_ref[rs], sinc_ref[rs]))

    @pl.when(plain)
    def _():
        project(None)


def _inproj(h, w, layer, q_gain, k_gain, tabs, cfg):
    m, d = h.shape
    n = w.shape[2]
    tn, tm = cfg["in_tn"], cfg["big_tm"]
    tab = pl.BlockSpec((tm, HEAD_DIM), lambda i, j: (i, 0), pipeline_mode=pl.Buffered(1))
    vec = pl.BlockSpec((1, HEAD_DIM), lambda i, j: (0, 0))
    kern = functools.partial(
        _inproj_kernel, tn=tn, q_tiles=cfg["a_q_w"] // tn, k_tiles=cfg["a_kv_w"] // tn,
        rope_c_ranges=[(cfg["off_cq"] // tn, (cfg["off_cq"] + 2 * cfg["c_qk_w"]) // tn)])
    return pl.pallas_call(
        kern,
        grid=(m // tm, n // tn),
        in_specs=[
            pl.BlockSpec((tm, d), lambda i, j: (i, 0), pipeline_mode=pl.Buffered(1)),
            pl.BlockSpec((None, d, tn), lambda i, j: (layer, 0, j)),
            vec, vec, tab, tab, tab, tab,
        ],
        out_specs=pl.BlockSpec((tm, tn), lambda i, j: (i, j)),
        out_shape=jax.ShapeDtypeStruct((m, n), BF16),
        compiler_params=_params("arbitrary", "arbitrary"),
        name="inproj",
    )(h, w, q_gain.reshape(1, HEAD_DIM), k_gain.reshape(1, HEAD_DIM), *tabs)


ONES_ROWS = 16
EXP_CHUNK = 256


def _attend_t(q, k, vt, c):
    st = _dot_nt(k, q)
    m = jnp.max(st, axis=0, keepdims=True)
    acc = _dot(vt, jnp.exp2((st - m) * c).astype(BF16))
    d = vt.shape[0] - ONES_ROWS
    return acc[:d] / acc[d:d + 1]


def _qk_max(q, k, st_ref):
    st = _dot_nt(k, q)
    st_ref[...] = st
    return jnp.max(st, axis=0, keepdims=True)


def _exp_pv(st_ref, pt_ref, vt, m, c):
    n = st_ref.shape[0]
    for c0 in range(0, n, EXP_CHUNK):
        c1 = min(c0 + EXP_CHUNK, n)
        pt_ref[c0:c1] = jnp.exp2((st_ref[c0:c1] - m) * c).astype(BF16)
    acc = _dot(vt, pt_ref[...])
    d = vt.shape[0] - ONES_ROWS
    return acc[:d] / acc[d:d + 1]


def _attend_tiles(n_iter, n_items, load_q, store_o, k, vt, st_ref, pt_ref, m_ref, c):
    def scores(i, j):
        m_ref[j] = _qk_max(load_q(i, j), k, st_ref.at[j])

    lead = n_items // 2
    for j in range(lead):
        scores(0, j)

    def body(i, carry):
        nxt = jnp.minimum(i + 1, n_iter - 1)
        for j in range(n_items):
            if j + lead < n_items:
                scores(i, j + lead)
            else:
                scores(nxt, j + lead - n_items)
            store_o(i, j, _exp_pv(st_ref.at[j], pt_ref.at[j % 2], vt, m_ref[j], c))
        return carry

    lax.fori_loop(0, n_iter, body, 0)


def _fill_vt(vt_ref, v_ref):
    vt_ref[:HEAD_DIM] = v_ref[0].T
    vt_ref[HEAD_DIM:] = jnp.ones((ONES_ROWS, vt_ref.shape[1]), BF16)


def _attn_scratch(t, n_items):
    return [pltpu.VMEM((HEAD_DIM + ONES_ROWS, t), BF16), pltpu.VMEM((n_items, t, ROW_TILE), F32),
            pltpu.VMEM((2, t, ROW_TILE), BF16), pltpu.VMEM((n_items, 1, ROW_TILE), F32)]


def _attn_a_kernel(q_ref, k_ref, v_ref, o_ref, vt_ref, st_ref, pt_ref, m_ref, *, n_ctx, group, c):
    t = k_ref.shape[1]
    tq = ROW_TILE
    _fill_vt(vt_ref, v_ref)
    heads = [slice(g * HEAD_DIM, (g + 1) * HEAD_DIM) for g in range(group)]

    for r0 in range(0, n_ctx, tq):
        for sl in heads:
            o = _attend_t(q_ref[0, r0:r0 + tq, sl], k_ref[0, :n_ctx], vt_ref[:, :n_ctx], c)
            o_ref[0, r0:r0 + tq, sl] = o.T.astype(BF16)

    def rows(i):
        return pl.ds(pl.multiple_of(n_ctx + i * tq, tq), tq)

    def load_q(i, g):
        return q_ref[0, rows(i), heads[g]]

    def store_o(i, g, o):
        o_ref[0, rows(i), heads[g]] = o.T.astype(BF16)

    _attend_tiles((t - n_ctx) // tq, group, load_q, store_o, k_ref[0], vt_ref[...], st_ref, pt_ref, m_ref, c)


def _attn_a(proj, cfg):
    b, t, _ = proj.shape
    gw = Q_GROUP * HEAD_DIM
    kvh = cfg["a_kv_w"] // HEAD_DIM
    k_blk = cfg["a_q_w"] // HEAD_DIM
    v_blk = (cfg["a_q_w"] + cfg["a_kv_w"]) // HEAD_DIM
    kern = functools.partial(_attn_a_kernel, n_ctx=cfg["n_ctx"], group=Q_GROUP, c=HEAD_DIM ** -0.5 * LOG2_E)
    return pl.pallas_call(
        kern,
        grid=(b, kvh),
        in_specs=[
            pl.BlockSpec((1, t, gw), lambda bi, hi: (bi, 0, hi)),
            pl.BlockSpec((1, t, HEAD_DIM), lambda bi, hi: (bi, 0, k_blk + hi)),
            pl.BlockSpec((1, t, HEAD_DIM), lambda bi, hi: (bi, 0, v_blk + hi)),
        ],
        out_specs=pl.BlockSpec((1, t, gw), lambda bi, hi: (bi, 0, hi)),
        out_shape=jax.ShapeDtypeStruct((b, t, cfg["a_q_w"]), BF16),
        scratch_shapes=_attn_scratch(t, Q_GROUP),
        compiler_params=_params("arbitrary", "arbitrary"),
        name="attn_a",
    )(proj, proj, proj)


def _attn_c_kernel(q_ref, k_ref, v_ref, l1_ref, l2_ref, l3_ref, l4_ref, gain_ref, o_ref,
                   vt_ref, st_ref, pt_ref, m_ref, *, n_ctx, c, lam_init):
    t = k_ref.shape[1]
    tq = ROW_TILE
    _fill_vt(vt_ref, v_ref)
    lam = (jnp.exp(jnp.sum(l1_ref[...] * l2_ref[...], axis=-1, keepdims=True))
           - jnp.exp(jnp.sum(l3_ref[...] * l4_ref[...], axis=-1, keepdims=True)) + lam_init)
    lane = lax.broadcasted_iota(jnp.int32, (tq, HEAD_DIM), 1)
    first = lane < HEAD_DIM // 2

    def component(q, comp):
        return jnp.where(first == (comp == 0), q, jnp.zeros_like(q))

    def finish(o1, o2):
        o = (o1 - lam * o2).T
        return (_rms(o, gain_ref[...]) * (1.0 - lam_init)).astype(BF16)

    for r0 in range(0, n_ctx, tq):
        q = q_ref[0, r0:r0 + tq]
        o1, o2 = [_attend_t(component(q, comp), k_ref[0, :n_ctx], vt_ref[:, :n_ctx], c) for comp in range(2)]
        o_ref[0, r0:r0 + tq] = finish(o1, o2)

    def rows(i, j):
        return pl.ds(pl.multiple_of(n_ctx + (2 * i + j // 2) * tq, tq), tq)

    def load_q(i, j):
        return component(q_ref[0, rows(i, j)], j % 2)

    pending = {}

    def store_o(i, j, o):
        if j % 2 == 0:
            pending[0] = o
        else:
            o_ref[0, rows(i, j)] = finish(pending.pop(0), o)

    _attend_tiles((t - n_ctx) // (2 * tq), 4, load_q, store_o, k_ref[0], vt_ref[...], st_ref, pt_ref, m_ref, c)


def _attn_c(proj, lams, subln, lam_init, cfg):
    b, t, _ = proj.shape
    heads = cfg["c_v_w"] // HEAD_DIM
    q_blk = cfg["off_cq"] // HEAD_DIM
    k_blk = q_blk + cfg["c_qk_w"] // HEAD_DIM
    v_blk = k_blk + cfg["c_qk_w"] // HEAD_DIM
    qk_dim = HEAD_DIM // 2
    lam_spec = pl.BlockSpec((1, qk_dim), lambda bi, hi: (0, 0))
    kern = functools.partial(_attn_c_kernel, n_ctx=cfg["n_ctx"], c=qk_dim ** -0.5 * LOG2_E, lam_init=lam_init)
    return pl.pallas_call(
        kern,
        grid=(b, heads),
        in_specs=[
            pl.BlockSpec((1, t, HEAD_DIM), lambda bi, hi: (bi, 0, q_blk + hi)),
            pl.BlockSpec((1, t, HEAD_DIM), lambda bi, hi: (bi, 0, k_blk + hi)),
            pl.BlockSpec((1, t, HEAD_DIM), lambda bi, hi: (bi, 0, v_blk + hi)),
            lam_spec, lam_spec, lam_spec, lam_spec,
            pl.BlockSpec((1, HEAD_DIM), lambda bi, hi: (0, 0)),
        ],
        out_specs=pl.BlockSpec((1, t, HEAD_DIM), lambda bi, hi: (bi, 0, hi)),
        out_shape=jax.ShapeDtypeStruct((b, t, cfg["c_v_w"]), BF16),
        scratch_shapes=_attn_scratch(t, 4),
        compiler_params=_params("arbitrary", "arbitrary"),
        name="attn_c",
    )(proj, proj, proj, *[v.reshape(1, qk_dim) for v in lams], subln.reshape(1, HEAD_DIM))


def _attn_b_kernel(q_ref, k_ref, v_ref, pair_ref, o_ref, bias_ref, v1_ref, *, n_ctx, blocks, pair_idx, q_rows,
                   k_rows, c):
    @pl.when(pl.program_id(1) == 0)
    def _():
        for case, case_idx in enumerate(pair_idx):
            for a, row_idx in enumerate(case_idx):
                for p, pair in enumerate(row_idx):
                    bias_ref[case, a * GRID_W:(a + 1) * GRID_W, 2 * p * GRID_W:2 * (p + 1) * GRID_W] = (
                        pair_ref[0, pair])

    v1_ref[:, :HEAD_DIM] = v_ref[0]
    v1_ref[:, HEAD_DIM:] = jnp.ones((v1_ref.shape[0], HEAD_DIM), BF16)
    k_ctx = k_ref[0, :n_ctx]
    v_ctx = v1_ref[:n_ctx]

    def finish(acc):
        return (acc[:, :HEAD_DIM] / acc[:, HEAD_DIM:]).astype(BF16)

    s = _dot_nt(q_ref[0, :n_ctx], k_ctx)
    p = jnp.exp2((s - jnp.max(s, axis=-1, keepdims=True)) * c)
    o_ref[0, :n_ctx] = finish(_dot(p.astype(BF16), v_ctx))
    for q0, k0, case in blocks:
        q = q_ref[0, n_ctx + q0:n_ctx + q0 + q_rows]
        win = slice(n_ctx + k0, n_ctx + k0 + k_rows)
        s_loc = _dot_nt(q, k_ref[0, win]) + bias_ref[case]
        s_ctx = _dot_nt(q, k_ctx)
        m = jnp.maximum(jnp.max(s_loc, axis=-1, keepdims=True), jnp.max(s_ctx, axis=-1, keepdims=True))
        acc = (_dot(jnp.exp2((s_loc - m) * c).astype(BF16), v1_ref[win])
               + _dot(jnp.exp2((s_ctx - m) * c).astype(BF16), v_ctx))
        o_ref[0, n_ctx + q0:n_ctx + q0 + q_rows] = finish(acc)


def _nbr_plan(seq):
    rows = seq // GRID_W
    kr = min(NA_ROWS, rows)
    q_rows_n = min(NA_ROWS, rows)
    win = min(2 * NA_ROWS, rows)
    cases, blocks = [], []
    for i in range(rows // q_rows_n):
        ws = int(np.clip(i * q_rows_n - kr // 2, 0, rows - win))
        drmap = np.full((q_rows_n, win), -1, np.int64)
        for a in range(q_rows_n):
            r = i * q_rows_n + a
            r0 = int(np.clip(r - kr // 2, 0, rows - kr))
            for w in range(win):
                kr_abs = ws + w
                if r0 <= kr_abs < r0 + kr:
                    drmap[a, w] = kr_abs - r + (NA_ROWS - 1)
        key = drmap.tobytes()
        keys = [c.tobytes() for c in cases]
        if key not in keys:
            cases.append(drmap)
            keys.append(key)
        blocks.append((i * q_rows_n * GRID_W, ws * GRID_W, keys.index(key)))
    masked = 2 * NA_ROWS - 1
    case_maps = np.where(np.stack(cases) < 0, masked, np.stack(cases))
    pair_list = sorted({(int(d0), int(d1)) for d0, d1 in case_maps.reshape(-1, 2)})
    pair_idx = [[[pair_list.index((int(row[2 * p]), int(row[2 * p + 1]))) for p in range(win // 2)]
                 for row in case] for case in case_maps]
    return blocks, np.asarray(pair_list), pair_idx, q_rows_n * GRID_W, win * GRID_W


def _nbr_bias(rpb, pairs, scale):
    depth, heads = rpb.shape[:2]
    c = np.arange(GRID_W)
    c0 = np.clip(c - NA_COLS // 2, 0, GRID_W - NA_COLS)
    col_mask = (c[None, :] >= c0[:, None]) & (c[None, :] < c0[:, None] + NA_COLS)
    col_idx = np.clip(c[None, :] - c[:, None] + NA_COLS - 1, 0, 2 * NA_COLS - 2)
    toe = jnp.where(col_mask, rpb.astype(F32)[:, :, :, col_idx] / scale, -jnp.inf)
    toe = jnp.concatenate([toe, jnp.full((depth, heads, 1, GRID_W, GRID_W), -jnp.inf, F32)], axis=2)
    return jnp.concatenate([jnp.take(toe, pairs[:, 0], axis=2), jnp.take(toe, pairs[:, 1], axis=2)], axis=-1)


def _attn_b(proj, pair_bias, layer, plan, cfg):
    b, t, _ = proj.shape
    heads = cfg["b_w"] // HEAD_DIM
    q_blk = cfg["off_bq"] // HEAD_DIM
    k_blk = q_blk + heads
    v_blk = k_blk + heads
    blocks, _, pair_idx, q_rows, k_rows = plan
    n_pair = pair_bias.shape[2]
    kern = functools.partial(_attn_b_kernel, n_ctx=cfg["n_ctx"], blocks=blocks, pair_idx=pair_idx,
                             q_rows=q_rows, k_rows=k_rows, c=HEAD_DIM ** -0.5 * LOG2_E)
    return pl.pallas_call(
        kern,
        grid=(heads, b),
        in_specs=[
            pl.BlockSpec((1, t, HEAD_DIM), lambda hi, bi: (bi, 0, q_blk + hi)),
            pl.BlockSpec((1, t, HEAD_DIM), lambda hi, bi: (bi, 0, k_blk + hi)),
            pl.BlockSpec((1, t, HEAD_DIM), lambda hi, bi: (bi, 0, v_blk + hi)),
            pl.BlockSpec((None, 1, n_pair, GRID_W, 2 * GRID_W), lambda hi, bi: (layer, hi, 0, 0, 0)),
        ],
        out_specs=pl.BlockSpec((1, t, HEAD_DIM), lambda hi, bi: (bi, 0, hi)),
        out_shape=jax.ShapeDtypeStruct((b, t, cfg["b_w"]), BF16),
        scratch_shapes=[pltpu.VMEM((len(pair_idx), q_rows, k_rows), F32), pltpu.VMEM((t, 2 * HEAD_DIM), BF16)],
        compiler_params=_params("arbitrary", "arbitrary"),
        name="attn_b",
    )(proj, proj, proj, pair_bias)


def _merge_kernel(g_ref, oa_ref, ob_ref, oc_ref, wga_ref, wgb_ref, wgc_ref, bga_ref, bgb_ref, bgc_ref,
                  wa_ref, wb_ref, wc_ref, o_ref):
    g_low = g_ref[...]
    acc = None
    for o_br, w_br, wg, bg in ((oa_ref, wa_ref, wga_ref, bga_ref), (ob_ref, wb_ref, wgb_ref, bgb_ref),
                               (oc_ref, wc_ref, wgc_ref, bgc_ref)):
        gate = jax.nn.sigmoid(_dot(g_low, wg[...].astype(BF16)) + bg[...])
        term = gate * _dot(o_br[...], w_br[...].astype(BF16))
        acc = term if acc is None else acc + term
    o_ref[...] = acc.astype(BF16)


def _merge(proj2, o_a, o_b, o_c, w_gate, b_gate, w_branch, layer, cfg):
    m = proj2.shape[0]
    d = w_branch.shape[2]
    rank = w_gate.shape[1]
    tm, tn = cfg["mm_tm"], cfg["mm_tn"]
    a_w, b_w, c_w = cfg["a_q_w"], cfg["b_w"], cfg["c_v_w"]
    g_blk = cfg["off_g"] // rank
    nb = d // tn
    gate_specs = [pl.BlockSpec((None, rank, tn), lambda i, j, br=br: (layer, 0, br * nb + j)) for br in range(3)]
    bias_specs = [pl.BlockSpec((None, 1, tn), lambda i, j, br=br: (layer, 0, br * nb + j)) for br in range(3)]
    return pl.pallas_call(
        _merge_kernel,
        grid=(m // tm, nb),
        in_specs=[
            pl.BlockSpec((tm, rank), lambda i, j: (i, g_blk)),
            pl.BlockSpec((tm, a_w), lambda i, j: (i, 0)),
            pl.BlockSpec((tm, b_w), lambda i, j: (i, 0)),
            pl.BlockSpec((tm, c_w), lambda i, j: (i, 0)),
            *gate_specs, *bias_specs,
            pl.BlockSpec((None, a_w, tn), lambda i, j: (layer, 0, j)),
            pl.BlockSpec((None, b_w, tn), lambda i, j: (layer, a_w // b_w, j)),
            pl.BlockSpec((None, c_w, tn), lambda i, j: (layer, (a_w + b_w) // c_w, j)),
        ],
        out_specs=pl.BlockSpec((tm, tn), lambda i, j: (i, j)),
        out_shape=jax.ShapeDtypeStruct((m, d), BF16),
        compiler_params=_params("arbitrary", "arbitrary"),
        name="merge",
    )(proj2, o_a, o_b, o_c, w_gate, w_gate, w_gate, b_gate, b_gate, b_gate, w_branch, w_branch, w_branch)


def _matmul_kernel(a_ref, w_ref, o_ref):
    w = w_ref[...]
    o_ref[...] = _dot(a_ref[...], w if w.dtype == BF16 else w.astype(BF16))


def _matmul_f32(a, w, layer, tm, tn, name, a_buffers=2):
    m, k = a.shape
    n = w.shape[2]
    return pl.pallas_call(
        _matmul_kernel,
        grid=(m // tm, n // tn),
        in_specs=[pl.BlockSpec((tm, k), lambda i, j: (i, 0), pipeline_mode=pl.Buffered(a_buffers)),
                  pl.BlockSpec((None, k, tn), lambda i, j: (layer, 0, j))],
        out_specs=pl.BlockSpec((tm, tn), lambda i, j: (i, j)),
        out_shape=jax.ShapeDtypeStruct((m, n), F32),
        compiler_params=_params("arbitrary", "arbitrary"),
        name=name,
    )(a, w)


def _ffn_in_kernel(h_ref, wg_ref, wu_ref, o_ref):
    h = h_ref[...]
    gate = _dot(h, wg_ref[...].astype(BF16))
    up = _dot(h, wu_ref[...].astype(BF16))
    o_ref[...] = (gate * jax.nn.sigmoid(gate) * up).astype(BF16)


def _ffn_in(h, w, layer, cfg):
    m, d = h.shape
    d_ff = w.shape[2] // 2
    tm, tn = cfg["big_tm"], cfg["ffn_tn"]
    up_blk = d_ff // tn
    return pl.pallas_call(
        _ffn_in_kernel,
        grid=(m // tm, d_ff // tn),
        in_specs=[pl.BlockSpec((tm, d), lambda i, j: (i, 0), pipeline_mode=pl.Buffered(1)),
                  pl.BlockSpec((None, d, tn), lambda i, j: (layer, 0, j)),
                  pl.BlockSpec((None, d, tn), lambda i, j: (layer, 0, up_blk + j))],
        out_specs=pl.BlockSpec((tm, tn), lambda i, j: (i, j)),
        out_shape=jax.ShapeDtypeStruct((m, d_ff), BF16),
        compiler_params=_params("arbitrary", "arbitrary"),
        name="ffn_in",
    )(h, w, w)


def _rope_tables(seq, n_ctx, batch, dim):
    t = np.arange(seq)
    rows = (t // GRID_W).astype(np.float32)
    cols = (t % GRID_W).astype(np.float32)
    n_pairs = dim // 4
    freqs = jnp.asarray(ROPE_THETA, F32) ** (-jnp.arange(n_pairs, dtype=F32) / n_pairs)
    ang = jnp.concatenate([jnp.asarray(rows)[:, None] * freqs, jnp.asarray(cols)[:, None] * freqs], axis=-1)
    sign = jnp.asarray(np.tile([-1.0, 1.0], dim // 2), F32)
    cos = jnp.tile(jnp.repeat(jnp.cos(ang), 2, axis=1), (1, HEAD_DIM // dim))
    sin = jnp.tile(jnp.repeat(jnp.sin(ang), 2, axis=1) * sign, (1, HEAD_DIM // dim))
    cos = jnp.concatenate([jnp.ones((n_ctx, HEAD_DIM), F32), cos], axis=0)
    sin = jnp.concatenate([jnp.zeros((n_ctx, HEAD_DIM), F32), sin], axis=0)
    return jnp.tile(cos, (batch, 1)), jnp.tile(sin, (batch, 1))


def _config(d, n_ctx, in_w, rank, d_ff):
    heads = d // HEAD_DIM
    a_q_w = (heads // 2) * HEAD_DIM
    a_kv_w = a_q_w // Q_GROUP
    b_w = (heads // 4) * HEAD_DIM
    c_qk_w = c_v_w = (heads // 4) * HEAD_DIM
    off_bq = a_q_w + 2 * a_kv_w
    off_cq = off_bq + 3 * b_w
    off_g = off_cq + 2 * c_qk_w + c_v_w
    assert off_g + rank == in_w and off_g % rank == 0
    assert a_q_w % b_w == 0 and (a_q_w + b_w) % c_v_w == 0
    in_tn = math.gcd(math.gcd(a_kv_w, b_w), 512)
    ffn_tn = math.gcd(d_ff, 512)
    return dict(n_ctx=n_ctx, a_q_w=a_q_w, a_kv_w=a_kv_w, b_w=b_w, c_qk_w=c_qk_w, c_v_w=c_v_w,
                off_bq=off_bq, off_cq=off_cq, off_g=off_g, in_tn=in_tn, ffn_tn=ffn_tn,
                mm_tn=min(512, d))


def kernel(x, c, ctx, c_ctx, ada_down, ada_up, ada_bias, norm_mix_pre, norm_mix_post, norm_ffn_pre,
           norm_ffn_post, w_in, a_q_norm, a_k_norm, b_rel_bias, c_lambda_q1, c_lambda_k1, c_lambda_q2,
           c_lambda_k2, c_subln, w_gate_up, b_gate, w_branch, w_out, w_ffn_in, w_ffn_out):
    batch, seq, d = x.shape
    n_ctx = ctx.shape[1]
    depth, _, in_w = w_in.shape
    rank = w_gate_up.shape[1]
    d_ff = w_ffn_out.shape[1]
    t = n_ctx + seq
    m = batch * t
    cfg = _config(d, n_ctx, in_w, rank, d_ff)
    assert n_ctx % ROW_TILE == 0 and seq % ROW_TILE == 0 and seq % (NA_ROWS * GRID_W) == 0
    cfg["mm_tm"] = math.gcd(m, MM_TILE_M)
    cfg["big_tm"] = max(m // k for k in range(1, m // ROW_ALIGN + 1)
                        if m % (k * ROW_ALIGN) == 0 and m // k <= MM_TILE_M_BIG)
    geom = (t // ROW_TILE, n_ctx // ROW_TILE, batch)

    w_ffn_out_b = w_ffn_out.astype(BF16)
    b_gate_r = b_gate.reshape(depth, 1, 3 * d)

    tabs = _rope_tables(seq, n_ctx, batch, HEAD_DIM) + _rope_tables(seq, n_ctx, batch, HEAD_DIM // 2)
    plan = _nbr_plan(seq)
    nbr_bias = _nbr_bias(b_rel_bias, plan[1], HEAD_DIM ** -0.5)

    rows = -(-(batch + 1) // 8) * 8
    cond = jnp.zeros((rows, d), F32).at[:batch].set(c).at[batch].set(c_ctx)
    mods = _adaln(cond, ada_down, ada_up, ada_bias).reshape(depth, 6, rows, 1, d)

    xs, h = _embed(x, ctx, norm_mix_pre[0], mods, geom)
    for l in range(depth):
        lam_init = 0.8 - 0.6 * math.exp(-0.3 * l)
        proj = _inproj(h, w_in, l, a_q_norm[l], a_k_norm[l], tabs, cfg)
        proj3 = proj.reshape(batch, t, in_w)
        o_a = _attn_a(proj3, cfg).reshape(m, -1)
        o_b = _attn_b(proj3, nbr_bias, l, plan, cfg).reshape(m, -1)
        o_c = _attn_c(proj3, (c_lambda_q1[l], c_lambda_k1[l], c_lambda_q2[l], c_lambda_k2[l]),
                      c_subln[l], lam_init, cfg).reshape(m, -1)
        merged = _merge(proj, o_a, o_b, o_c, w_gate_up, b_gate_r, w_branch, l, cfg)
        mix = _matmul_f32(merged, w_out, l, cfg["big_tm"], cfg["mm_tn"], "out_proj", a_buffers=1)
        xs, h = _resid(xs, mix, norm_mix_post[l], norm_ffn_pre[l], mods, l, 2, l, 3, 4, geom)
        act = _ffn_in(h, w_ffn_in, l, cfg)
        f = _matmul_f32(act, w_ffn_out_b, l, min(512, cfg["mm_tm"]), cfg["mm_tn"], "ffn_out")
        if l + 1 < depth:
            xs, h = _resid(xs, f, norm_ffn_post[l], norm_mix_pre[l + 1], mods, l, 5, l + 1, 0, 1, geom)
    return _resid_out(xs, f, norm_ffn_post[depth - 1], mods, depth - 1, 5, geom).reshape(batch, seq, d)
```

```python
import functools
import math

import jax
import jax.numpy as jnp
import numpy as np
from jax import lax
from jax.experimental import pallas as pl
from jax.experimental.pallas import tpu as pltpu

HEAD_DIM = 128
GRID_W = 64
NA_ROWS = 8
NA_COLS = 16
Q_GROUP = 4
ROPE_THETA = 10000.0
EPS = 1e-6
LOG2_E = math.log2(math.e)
ROW_TILE = 256
MM_TILE_M = 1024
MM_TILE_M_BIG = 2304
SUB_TILES = 8
ROW_ALIGN = 128
VMEM_LIMIT = 58 * 1024 * 1024
BF16 = jnp.bfloat16
F32 = jnp.float32


def _params(*sem):
    return pltpu.CompilerParams(dimension_semantics=sem, vmem_limit_bytes=VMEM_LIMIT)


def _dot(a, b):
    return jnp.dot(a, b, preferred_element_type=F32)


def _dot_nt(a, b):
    return lax.dot_general(a, b, (((1,), (1,)), ((), ())), preferred_element_type=F32)


def _rms(x, gain):
    return x * lax.rsqrt(jnp.mean(x * x, axis=-1, keepdims=True) + EPS) * gain


def _adaln_kernel(cond_ref, down_ref, up_ref, bias_ref, o_ref):
    cond = cond_ref[...]
    act = (cond * jax.nn.sigmoid(cond)).astype(BF16)
    low = _dot(act, down_ref[0].astype(BF16))
    o_ref[0, 0] = _dot(low.astype(BF16), up_ref[0].astype(BF16)) + bias_ref[0]


def _adaln(cond, ada_down, ada_up, ada_bias):
    depth, d, rank = ada_down.shape
    rows = cond.shape[0]
    return pl.pallas_call(
        _adaln_kernel,
        grid=(depth, 6),
        in_specs=[
            pl.BlockSpec((rows, d), lambda l, k: (0, 0)),
            pl.BlockSpec((1, d, rank), lambda l, k: (l, 0, 0)),
            pl.BlockSpec((1, rank, d), lambda l, k: (l, 0, k)),
            pl.BlockSpec((1, 1, d), lambda l, k: (l, 0, k)),
        ],
        out_specs=pl.BlockSpec((1, 1, rows, d), lambda l, k: (l, k, 0, 0)),
        out_shape=jax.ShapeDtypeStruct((depth, 6, rows, d), F32),
        compiler_params=_params("arbitrary", "arbitrary"),
        name="adaln",
    )(cond, ada_down, ada_up, ada_bias.reshape(depth, 1, 6 * d))


def _mod_row(i, tiles_per_batch, ctx_tiles, batch):
    return jnp.where(i % tiles_per_batch < ctx_tiles, batch, i // tiles_per_batch)


def _embed_kernel(x_ref, ctx_ref, gain_ref, shift_ref, scale_ref, xs_ref, h_ref, *, tiles_per_batch, ctx_tiles):
    def emit(src):
        xs_ref[...] = src
        h_ref[...] = (_rms(src, gain_ref[...]) * (1.0 + scale_ref[0, 0, 0]) + shift_ref[0, 0, 0]).astype(BF16)

    is_ctx = pl.program_id(0) % tiles_per_batch < ctx_tiles

    @pl.when(is_ctx)
    def _():
        emit(ctx_ref[0])

    @pl.when(jnp.logical_not(is_ctx))
    def _():
        emit(x_ref[0])


def _embed(x, ctx, gain, mods, geom):
    batch, seq, d = x.shape
    tpb, ctx_tiles, _ = geom
    m = batch * tpb * ROW_TILE
    row = functools.partial(_mod_row, tiles_per_batch=tpb, ctx_tiles=ctx_tiles, batch=batch)
    tile = pl.BlockSpec((ROW_TILE, d), lambda i: (i, 0))
    kern = functools.partial(_embed_kernel, tiles_per_batch=tpb, ctx_tiles=ctx_tiles)
    return pl.pallas_call(
        kern,
        grid=(m // ROW_TILE,),
        in_specs=[
            pl.BlockSpec((1, ROW_TILE, d), lambda i: (i // tpb, jnp.maximum(i % tpb - ctx_tiles, 0), 0)),
            pl.BlockSpec((1, ROW_TILE, d), lambda i: (i // tpb, jnp.minimum(i % tpb, ctx_tiles - 1), 0)),
            pl.BlockSpec((1, d), lambda i: (0, 0)),
            pl.BlockSpec((1, 1, 1, 1, d), lambda i: (0, 0, row(i), 0, 0)),
            pl.BlockSpec((1, 1, 1, 1, d), lambda i: (0, 1, row(i), 0, 0)),
        ],
        out_specs=[tile, tile],
        out_shape=[jax.ShapeDtypeStruct((m, d), F32), jax.ShapeDtypeStruct((m, d), BF16)],
        compiler_params=_params("arbitrary"),
        name="embed",
    )(x, ctx, gain.reshape(1, d), mods, mods)


def _resid_kernel(x_ref, y_ref, post_ref, gate_ref, pre_ref, shift_ref, scale_ref, xo_ref, h_ref):
    x_new = x_ref[...] + gate_ref[0, 0, 0] * _rms(y_ref[...], post_ref[...])
    xo_ref[...] = x_new
    h = _rms(x_new, pre_ref[...])
    h_ref[...] = (h * (1.0 + scale_ref[0, 0, 0]) + shift_ref[0, 0, 0]).astype(BF16)


def _resid(xs, y, post_gain, pre_gain, mods, layer, k_gate, mod_layer, k_shift, k_scale, geom):
    m, d = xs.shape
    tpb, ctx_tiles, batch = geom
    row = functools.partial(_mod_row, tiles_per_batch=tpb, ctx_tiles=ctx_tiles, batch=batch)
    tile = pl.BlockSpec((ROW_TILE, d), lambda i: (i, 0))
    vec = pl.BlockSpec((1, d), lambda i: (0, 0))
    return pl.pallas_call(
        _resid_kernel,
        grid=(m // ROW_TILE,),
        in_specs=[
            tile, tile, vec,
            pl.BlockSpec((1, 1, 1, 1, d), lambda i: (layer, k_gate, row(i), 0, 0)),
            vec,
            pl.BlockSpec((1, 1, 1, 1, d), lambda i: (mod_layer, k_shift, row(i), 0, 0)),
            pl.BlockSpec((1, 1, 1, 1, d), lambda i: (mod_layer, k_scale, row(i), 0, 0)),
        ],
        out_specs=[tile, tile],
        out_shape=[jax.ShapeDtypeStruct((m, d), F32), jax.ShapeDtypeStruct((m, d), BF16)],
        compiler_params=_params("arbitrary"),
        name="resid",
    )(xs, y, post_gain.reshape(1, d), mods, pre_gain.reshape(1, d), mods, mods)


def _resid_out_kernel(x_ref, y_ref, post_ref, gate_ref, o_ref):
    o_ref[...] = x_ref[...] + gate_ref[0, 0, 0] * _rms(y_ref[...], post_ref[...])


def _resid_out(xs, y, post_gain, mods, layer, k_gate, geom):
    m, d = xs.shape
    tpb, ctx_tiles, batch = geom
    lat = tpb - ctx_tiles
    row = functools.partial(_mod_row, tiles_per_batch=tpb, ctx_tiles=ctx_tiles, batch=batch)
    tile = pl.BlockSpec((ROW_TILE, d), lambda i: (i, 0))
    return pl.pallas_call(
        _resid_out_kernel,
        grid=(m // ROW_TILE,),
        in_specs=[
            tile, tile, pl.BlockSpec((1, d), lambda i: (0, 0)),
            pl.BlockSpec((1, 1, 1, 1, d), lambda i: (layer, k_gate, row(i), 0, 0)),
        ],
        out_specs=pl.BlockSpec((ROW_TILE, d), lambda i: ((i // tpb) * lat + jnp.maximum(i % tpb - ctx_tiles, 0), 0)),
        out_shape=jax.ShapeDtypeStruct((batch * lat * ROW_TILE, d), F32),
        compiler_params=_params("arbitrary"),
        name="resid_out",
    )(xs, y, post_gain.reshape(1, d), mods)


def _rope(y, cos, sin):
    lane = lax.broadcasted_iota(jnp.int32, y.shape, 1)
    partner = jnp.where(lane % 2 == 0, pltpu.roll(y, HEAD_DIM - 1, 1), pltpu.roll(y, 1, 1))
    return y * cos + partner * sin


def _inproj_kernel(h_ref, w_ref, qg_ref, kg_ref, cosa_ref, sina_ref, cosc_ref, sinc_ref, o_ref,
                   *, tn, q_tiles, k_tiles, rope_c_ranges):
    j = pl.program_id(1)
    heads = tn // HEAD_DIM
    n_sub = max(k for k in range(1, SUB_TILES + 1) if h_ref.shape[0] % (16 * k) == 0)
    sub_rows = h_ref.shape[0] // n_sub

    def project(fn):
        for s in range(n_sub):
            rs = slice(s * sub_rows, (s + 1) * sub_rows)
            acc = _dot(h_ref[rs], w_ref[...].astype(BF16))
            if fn is None:
                o_ref[rs] = acc.astype(BF16)
                continue
            for hh in range(heads):
                sl = slice(hh * HEAD_DIM, (hh + 1) * HEAD_DIM)
                o_ref[rs, sl] = fn(acc[:, sl], rs).astype(BF16)

    is_q = j < q_tiles
    is_k = jnp.logical_and(j >= q_tiles, j < q_tiles + k_tiles)
    is_c = functools.reduce(jnp.logical_or,
                            [jnp.logical_and(j >= lo, j < hi) for lo, hi in rope_c_ranges])
    plain = jnp.logical_not(jnp.logical_or(jnp.logical_or(is_q, is_k), is_c))

    @pl.when(is_q)
    def _():
        project(lambda a, rs: _rope(_rms(a, qg_ref[...]), cosa_ref[rs], sina_ref[rs]))

    @pl.when(is_k)
    def _():
        project(lambda a, rs: _rope(_rms(a, kg_ref[...]), cosa_ref[rs], sina_ref[rs]))

    @pl.when(is_c)
    def _():
        project(lambda a, rs: _rope(a, cosc_ref[rs], sinc_ref[rs]))

    @pl.when(plain)
    def _():
        project(None)


def _inproj(h, w, layer, q_gain, k_gain, tabs, cfg):
    m, d = h.shape
    n = w.shape[2]
    tn, tm = cfg["in_tn"], cfg["big_tm"]
    tab = pl.BlockSpec((tm, HEAD_DIM), lambda i, j: (i, 0), pipeline_mode=pl.Buffered(1))
    vec = pl.BlockSpec((1, HEAD_DIM), lambda i, j: (0, 0))
    kern = functools.partial(
        _inproj_kernel, tn=tn, q_tiles=cfg["a_q_w"] // tn, k_tiles=cfg["a_kv_w"] // tn,
        rope_c_ranges=[(cfg["off_cq"] // tn, (cfg["off_cq"] + 2 * cfg["c_qk_w"]) // tn)])
    return pl.pallas_call(
        kern,
        grid=(m // tm, n // tn),
        in_specs=[
            pl.BlockSpec((tm, d), lambda i, j: (i, 0), pipeline_mode=pl.Buffered(1)),
            pl.BlockSpec((None, d, tn), lambda i, j: (layer, 0, j)),
            vec, vec, tab, tab, tab, tab,
        ],
        out_specs=pl.BlockSpec((tm, tn), lambda i, j: (i, j)),
        out_shape=jax.ShapeDtypeStruct((m, n), BF16),
        compiler_params=_params("arbitrary", "arbitrary"),
        name="inproj",
    )(h, w, q_gain.reshape(1, HEAD_DIM), k_gain.reshape(1, HEAD_DIM), *tabs)


ONES_ROWS = 16
EXP_CHUNK = 256


def _attend_t(q, k, vt, c):
    st = _dot_nt(k, q)
    m = jnp.max(st, axis=0, keepdims=True)
    acc = _dot(vt, jnp.exp2((st - m) * c).astype(BF16))
    d = vt.shape[0] - ONES_ROWS
    return acc[:d] / acc[d:d + 1]


def _qk_max(q, k, st_ref):
    st = _dot_nt(k, q)
    st_ref[...] = st
    return jnp.max(st, axis=0, keepdims=True)


def _exp_pv(st_ref, pt_ref, vt, m, c):
    n = st_ref.shape[0]
    for c0 in range(0, n, EXP_CHUNK):
        c1 = min(c0 + EXP_CHUNK, n)
        pt_ref[c0:c1] = jnp.exp2((st_ref[c0:c1] - m) * c).astype(BF16)
    acc = _dot(vt, pt_ref[...])
    d = vt.shape[0] - ONES_ROWS
    return acc[:d] / acc[d:d + 1]


def _attend_tiles(n_iter, n_items, load_q, store_o, k, vt, st_ref, pt_ref, m_ref, c):
    def scores(i, j):
        m_ref[j] = _qk_max(load_q(i, j), k, st_ref.at[j])

    lead = n_items // 2
    for j in range(lead):
        scores(0, j)

    def body(i, carry):
        nxt = jnp.minimum(i + 1, n_iter - 1)
        for j in range(n_items):
            if j + lead < n_items:
                scores(i, j + lead)
            else:
                scores(nxt, j + lead - n_items)
            store_o(i, j, _exp_pv(st_ref.at[j], pt_ref.at[j % 2], vt, m_ref[j], c))
        return carry

    lax.fori_loop(0, n_iter, body, 0)


def _fill_vt(vt_ref, v_ref):
    vt_ref[:HEAD_DIM] = v_ref[0].T
    vt_ref[HEAD_DIM:] = jnp.ones((ONES_ROWS, vt_ref.shape[1]), BF16)


def _attn_scratch(t, n_items):
    return [pltpu.VMEM((HEAD_DIM + ONES_ROWS, t), BF16), pltpu.VMEM((n_items, t, ROW_TILE), F32),
            pltpu.VMEM((2, t, ROW_TILE), BF16), pltpu.VMEM((n_items, 1, ROW_TILE), F32)]


def _attn_a_kernel(q_ref, k_ref, v_ref, o_ref, vt_ref, st_ref, pt_ref, m_ref, *, n_ctx, group, c):
    t = k_ref.shape[1]
    tq = ROW_TILE
    _fill_vt(vt_ref, v_ref)
    heads = [slice(g * HEAD_DIM, (g + 1) * HEAD_DIM) for g in range(group)]

    for r0 in range(0, n_ctx, tq):
        for sl in heads:
            o = _attend_t(q_ref[0, r0:r0 + tq, sl], k_ref[0, :n_ctx], vt_ref[:, :n_ctx], c)
            o_ref[0, r0:r0 + tq, sl] = o.T.astype(BF16)

    def rows(i):
        return pl.ds(pl.multiple_of(n_ctx + i * tq, tq), tq)

    def load_q(i, g):
        return q_ref[0, rows(i), heads[g]]

    def store_o(i, g, o):
        o_ref[0, rows(i), heads[g]] = o.T.astype(BF16)

    _attend_tiles((t - n_ctx) // tq, group, load_q, store_o, k_ref[0], vt_ref[...], st_ref, pt_ref, m_ref, c)


def _attn_a(proj, cfg):
    b, t, _ = proj.shape
    gw = Q_GROUP * HEAD_DIM
    kvh = cfg["a_kv_w"] // HEAD_DIM
    k_blk = cfg["a_q_w"] // HEAD_DIM
    v_blk = (cfg["a_q_w"] + cfg["a_kv_w"]) // HEAD_DIM
    kern = functools.partial(_attn_a_kernel, n_ctx=cfg["n_ctx"], group=Q_GROUP, c=HEAD_DIM ** -0.5 * LOG2_E)
    return pl.pallas_call(
        kern,
        grid=(b, kvh),
        in_specs=[
            pl.BlockSpec((1, t, gw), lambda bi, hi: (bi, 0, hi)),
            pl.BlockSpec((1, t, HEAD_DIM), lambda bi, hi: (bi, 0, k_blk + hi)),
            pl.BlockSpec((1, t, HEAD_DIM), lambda bi, hi: (bi, 0, v_blk + hi)),
        ],
        out_specs=pl.BlockSpec((1, t, gw), lambda bi, hi: (bi, 0, hi)),
        out_shape=jax.ShapeDtypeStruct((b, t, cfg["a_q_w"]), BF16),
        scratch_shapes=_attn_scratch(t, Q_GROUP),
        compiler_params=_params("arbitrary", "arbitrary"),
        name="attn_a",
    )(proj, proj, proj)


def _attn_c_kernel(q_ref, k_ref, v_ref, l1_ref, l2_ref, l3_ref, l4_ref, gain_ref, o_ref,
                   vt_ref, st_ref, pt_ref, m_ref, *, n_ctx, c, lam_init):
    t = k_ref.shape[1]
    tq = ROW_TILE
    _fill_vt(vt_ref, v_ref)
    lam = (jnp.exp(jnp.sum(l1_ref[...] * l2_ref[...], axis=-1, keepdims=True))
           - jnp.exp(jnp.sum(l3_ref[...] * l4_ref[...], axis=-1, keepdims=True)) + lam_init)
    lane = lax.broadcasted_iota(jnp.int32, (tq, HEAD_DIM), 1)
    first = lane < HEAD_DIM // 2

    def component(q, comp):
        return jnp.where(first == (comp == 0), q, jnp.zeros_like(q))

    def finish(o1, o2):
        o = (o1 - lam * o2).T
        return (_rms(o, gain_ref[...]) * (1.0 - lam_init)).astype(BF16)

    for r0 in range(0, n_ctx, tq):
        q = q_ref[0, r0:r0 + tq]
        o1, o2 = [_attend_t(component(q, comp), k_ref[0, :n_ctx], vt_ref[:, :n_ctx], c) for comp in range(2)]
        o_ref[0, r0:r0 + tq] = finish(o1, o2)

    def rows(i, j):
        return pl.ds(pl.multiple_of(n_ctx + (2 * i + j // 2) * tq, tq), tq)

    def load_q(i, j):
        return component(q_ref[0, rows(i, j)], j % 2)

    pending = {}

    def store_o(i, j, o):
        if j % 2 == 0:
            pending[0] = o
        else:
            o_ref[0, rows(i, j)] = finish(pending.pop(0), o)

    _attend_tiles((t - n_ctx) // (2 * tq), 4, load_q, store_o, k_ref[0], vt_ref[...], st_ref, pt_ref, m_ref, c)


def _attn_c(proj, lams, subln, lam_init, cfg):
    b, t, _ = proj.shape
    heads = cfg["c_v_w"] // HEAD_DIM
    q_blk = cfg["off_cq"] // HEAD_DIM
    k_blk = q_blk + cfg["c_qk_w"] // HEAD_DIM
    v_blk = k_blk + cfg["c_qk_w"] // HEAD_DIM
    qk_dim = HEAD_DIM // 2
    lam_spec = pl.BlockSpec((1, qk_dim), lambda bi, hi: (0, 0))
    kern = functools.partial(_attn_c_kernel, n_ctx=cfg["n_ctx"], c=qk_dim ** -0.5 * LOG2_E, lam_init=lam_init)
    return pl.pallas_call(
        kern,
        grid=(b, heads),
        in_specs=[
            pl.BlockSpec((1, t, HEAD_DIM), lambda bi, hi: (bi, 0, q_blk + hi)),
            pl.BlockSpec((1, t, HEAD_DIM), lambda bi, hi: (bi, 0, k_blk + hi)),
            pl.BlockSpec((1, t, HEAD_DIM), lambda bi, hi: (bi, 0, v_blk + hi)),
            lam_spec, lam_spec, lam_spec, lam_spec,
            pl.BlockSpec((1, HEAD_DIM), lambda bi, hi: (0, 0)),
        ],
        out_specs=pl.BlockSpec((1, t, HEAD_DIM), lambda bi, hi: (bi, 0, hi)),
        out_shape=jax.ShapeDtypeStruct((b, t, cfg["c_v_w"]), BF16),
        scratch_shapes=_attn_scratch(t, 4),
        compiler_params=_params("arbitrary", "arbitrary"),
        name="attn_c",
    )(proj, proj, proj, *[v.reshape(1, qk_dim) for v in lams], subln.reshape(1, HEAD_DIM))


def _attn_b_kernel(q_ref, k_ref, v_ref, pair_ref, o_ref, bias_ref, v1_ref, *, n_ctx, blocks, pair_idx, q_rows,
                   k_rows, c):
    @pl.when(pl.program_id(1) == 0)
    def _():
        for case, case_idx in enumerate(pair_idx):
            for a, row_idx in enumerate(case_idx):
                for p, pair in enumerate(row_idx):
                    bias_ref[case, a * GRID_W:(a + 1) * GRID_W, 2 * p * GRID_W:2 * (p + 1) * GRID_W] = (
                        pair_ref[0, pair])

    v1_ref[:, :HEAD_DIM] = v_ref[0]
    v1_ref[:, HEAD_DIM:] = jnp.ones((v1_ref.shape[0], HEAD_DIM), BF16)
    k_ctx = k_ref[0, :n_ctx]
    v_ctx = v1_ref[:n_ctx]

    def finish(acc):
        return (acc[:, :HEAD_DIM] / acc[:, HEAD_DIM:]).astype(BF16)

    s = _dot_nt(q_ref[0, :n_ctx], k_ctx)
    p = jnp.exp2((s - jnp.max(s, axis=-1, keepdims=True)) * c)
    o_ref[0, :n_ctx] = finish(_dot(p.astype(BF16), v_ctx))
    for q0, k0, case in blocks:
        q = q_ref[0, n_ctx + q0:n_ctx + q0 + q_rows]
        win = slice(n_ctx + k0, n_ctx + k0 + k_rows)
        s_loc = _dot_nt(q, k_ref[0, win]) + bias_ref[case]
        s_ctx = _dot_nt(q, k_ctx)
        m = jnp.maximum(jnp.max(s_loc, axis=-1, keepdims=True), jnp.max(s_ctx, axis=-1, keepdims=True))
        acc = (_dot(jnp.exp2((s_loc - m) * c).astype(BF16), v1_ref[win])
               + _dot(jnp.exp2((s_ctx - m) * c).astype(BF16), v_ctx))
        o_ref[0, n_ctx + q0:n_ctx + q0 + q_rows] = finish(acc)


def _nbr_plan(seq):
    rows = seq // GRID_W
    kr = min(NA_ROWS, rows)
    q_rows_n = min(NA_ROWS, rows)
    win = min(2 * NA_ROWS, rows)
    cases, blocks = [], []
    for i in range(rows // q_rows_n):
        ws = int(np.clip(i * q_rows_n - kr // 2, 0, rows - win))
        drmap = np.full((q_rows_n, win), -1, np.int64)
        for a in range(q_rows_n):
            r = i * q_rows_n + a
            r0 = int(np.clip(r - kr // 2, 0, rows - kr))
            for w in range(win):
                kr_abs = ws + w
                if r0 <= kr_abs < r0 + kr:
                    drmap[a, w] = kr_abs - r + (NA_ROWS - 1)
        key = drmap.tobytes()
        keys = [c.tobytes() for c in cases]
        if key not in keys:
            cases.append(drmap)
            keys.append(key)
        blocks.append((i * q_rows_n * GRID_W, ws * GRID_W, keys.index(key)))
    masked = 2 * NA_ROWS - 1
    case_maps = np.where(np.stack(cases) < 0, masked, np.stack(cases))
    pair_list = sorted({(int(d0), int(d1)) for d0, d1 in case_maps.reshape(-1, 2)})
    pair_idx = [[[pair_list.index((int(row[2 * p]), int(row[2 * p + 1]))) for p in range(win // 2)]
                 for row in case] for case in case_maps]
    return blocks, np.asarray(pair_list), pair_idx, q_rows_n * GRID_W, win * GRID_W


def _nbr_bias(rpb, pairs, scale):
    depth, heads = rpb.shape[:2]
    c = np.arange(GRID_W)
    c0 = np.clip(c - NA_COLS // 2, 0, GRID_W - NA_COLS)
    col_mask = (c[None, :] >= c0[:, None]) & (c[None, :] < c0[:, None] + NA_COLS)
    col_idx = np.clip(c[None, :] - c[:, None] + NA_COLS - 1, 0, 2 * NA_COLS - 2)
    toe = jnp.where(col_mask, rpb.astype(F32)[:, :, :, col_idx] / scale, -jnp.inf)
    toe = jnp.concatenate([toe, jnp.full((depth, heads, 1, GRID_W, GRID_W), -jnp.inf, F32)], axis=2)
    return jnp.concatenate([jnp.take(toe, pairs[:, 0], axis=2), jnp.take(toe, pairs[:, 1], axis=2)], axis=-1)


def _attn_b(proj, pair_bias, layer, plan, cfg):
    b, t, _ = proj.shape
    heads = cfg["b_w"] // HEAD_DIM
    q_blk = cfg["off_bq"] // HEAD_DIM
    k_blk = q_blk + heads
    v_blk = k_blk + heads
    blocks, _, pair_idx, q_rows, k_rows = plan
    n_pair = pair_bias.shape[2]
    kern = functools.partial(_attn_b_kernel, n_ctx=cfg["n_ctx"], blocks=blocks, pair_idx=pair_idx,
                             q_rows=q_rows, k_rows=k_rows, c=HEAD_DIM ** -0.5 * LOG2_E)
    return pl.pallas_call(
        kern,
        grid=(heads, b),
        in_specs=[
            pl.BlockSpec((1, t, HEAD_DIM), lambda hi, bi: (bi, 0, q_blk + hi)),
            pl.BlockSpec((1, t, HEAD_DIM), lambda hi, bi: (bi, 0, k_blk + hi)),
            pl.BlockSpec((1, t, HEAD_DIM), lambda hi, bi: (bi, 0, v_blk + hi)),
            pl.BlockSpec((None, 1, n_pair, GRID_W, 2 * GRID_W), lambda hi, bi: (layer, hi, 0, 0, 0)),
        ],
        out_specs=pl.BlockSpec((1, t, HEAD_DIM), lambda hi, bi: (bi, 0, hi)),
        out_shape=jax.ShapeDtypeStruct((b, t, cfg["b_w"]), BF16),
        scratch_shapes=[pltpu.VMEM((len(pair_idx), q_rows, k_rows), F32), pltpu.VMEM((t, 2 * HEAD_DIM), BF16)],
        compiler_params=_params("arbitrary", "arbitrary"),
        name="attn_b",
    )(proj, proj, proj, pair_bias)


def _merge_kernel(g_ref, oa_ref, ob_ref, oc_ref, wga_ref, wgb_ref, wgc_ref, bga_ref, bgb_ref, bgc_ref,
                  wa_ref, wb_ref, wc_ref, o_ref):
    g_low = g_ref[...]
    acc = None
    for o_br, w_br, wg, bg in ((oa_ref, wa_ref, wga_ref, bga_ref), (ob_ref, wb_ref, wgb_ref, bgb_ref),
                               (oc_ref, wc_ref, wgc_ref, bgc_ref)):
        gate = jax.nn.sigmoid(_dot(g_low, wg[...].astype(BF16)) + bg[...])
        term = gate * _dot(o_br[...], w_br[...].astype(BF16))
        acc = term if acc is None else acc + term
    o_ref[...] = acc.astype(BF16)


def _merge(proj2, o_a, o_b, o_c, w_gate, b_gate, w_branch, layer, cfg):
    m = proj2.shape[0]
    d = w_branch.shape[2]
    rank = w_gate.shape[1]
    tm, tn = cfg["mm_tm"], cfg["mm_tn"]
    a_w, b_w, c_w = cfg["a_q_w"], cfg["b_w"], cfg["c_v_w"]
    g_blk = cfg["off_g"] // rank
    nb = d // tn
    gate_specs = [pl.BlockSpec((None, rank, tn), lambda i, j, br=br: (layer, 0, br * nb + j)) for br in range(3)]
    bias_specs = [pl.BlockSpec((None, 1, tn), lambda i, j, br=br: (layer, 0, br * nb + j)) for br in range(3)]
    return pl.pallas_call(
        _merge_kernel,
        grid=(m // tm, nb),
        in_specs=[
            pl.BlockSpec((tm, rank), lambda i, j: (i, g_blk)),
            pl.BlockSpec((tm, a_w), lambda i, j: (i, 0)),
            pl.BlockSpec((tm, b_w), lambda i, j: (i, 0)),
            pl.BlockSpec((tm, c_w), lambda i, j: (i, 0)),
            *gate_specs, *bias_specs,
            pl.BlockSpec((None, a_w, tn), lambda i, j: (layer, 0, j)),
            pl.BlockSpec((None, b_w, tn), lambda i, j: (layer, a_w // b_w, j)),
            pl.BlockSpec((None, c_w, tn), lambda i, j: (layer, (a_w + b_w) // c_w, j)),
        ],
        out_specs=pl.BlockSpec((tm, tn), lambda i, j: (i, j)),
        out_shape=jax.ShapeDtypeStruct((m, d), BF16),
        compiler_params=_params("arbitrary", "arbitrary"),
        name="merge",
    )(proj2, o_a, o_b, o_c, w_gate, w_gate, w_gate, b_gate, b_gate, b_gate, w_branch, w_branch, w_branch)


def _matmul_kernel(a_ref, w_ref, o_ref):
    w = w_ref[...]
    o_ref[...] = _dot(a_ref[...], w if w.dtype == BF16 else w.astype(BF16))


def _matmul_f32(a, w, layer, tm, tn, name, a_buffers=2):
    m, k = a.shape
    n = w.shape[2]
    return pl.pallas_call(
        _matmul_kernel,
        grid=(m // tm, n // tn),
        in_specs=[pl.BlockSpec((tm, k), lambda i, j: (i, 0), pipeline_mode=pl.Buffered(a_buffers)),
                  pl.BlockSpec((None, k, tn), lambda i, j: (layer, 0, j))],
        out_specs=pl.BlockSpec((tm, tn), lambda i, j: (i, j)),
        out_shape=jax.ShapeDtypeStruct((m, n), F32),
        compiler_params=_params("arbitrary", "arbitrary"),
        name=name,
    )(a, w)


def _ffn_in_kernel(h_ref, wg_ref, wu_ref, o_ref):
    h = h_ref[...]
    gate = _dot(h, wg_ref[...].astype(BF16))
    up = _dot(h, wu_ref[...].astype(BF16))
    o_ref[...] = (gate * jax.nn.sigmoid(gate) * up).astype(BF16)


def _ffn_in(h, w, layer, cfg):
    m, d = h.shape
    d_ff = w.shape[2] // 2
    tm, tn = cfg["big_tm"], cfg["ffn_tn"]
    up_blk = d_ff // tn
    return pl.pallas_call(
        _ffn_in_kernel,
        grid=(m // tm, d_ff // tn),
        in_specs=[pl.BlockSpec((tm, d), lambda i, j: (i, 0)),
                  pl.BlockSpec((None, d, tn), lambda i, j: (layer, 0, j)),
                  pl.BlockSpec((None, d, tn), lambda i, j: (layer, 0, up_blk + j))],
        out_specs=pl.BlockSpec((tm, tn), lambda i, j: (i, j)),
        out_shape=jax.ShapeDtypeStruct((m, d_ff), BF16),
        compiler_params=_params("arbitrary", "arbitrary"),
        name="ffn_in",
    )(h, w, w)


def _rope_tables(seq, n_ctx, batch, dim):
    t = np.arange(seq)
    rows = (t // GRID_W).astype(np.float32)
    cols = (t % GRID_W).astype(np.float32)
    n_pairs = dim // 4
    freqs = jnp.asarray(ROPE_THETA, F32) ** (-jnp.arange(n_pairs, dtype=F32) / n_pairs)
    ang = jnp.concatenate([jnp.asarray(rows)[:, None] * freqs, jnp.asarray(cols)[:, None] * freqs], axis=-1)
    sign = jnp.asarray(np.tile([-1.0, 1.0], dim // 2), F32)
    cos = jnp.tile(jnp.repeat(jnp.cos(ang), 2, axis=1), (1, HEAD_DIM // dim))
    sin = jnp.tile(jnp.repeat(jnp.sin(ang), 2, axis=1) * sign, (1, HEAD_DIM // dim))
    cos = jnp.concatenate([jnp.ones((n_ctx, HEAD_DIM), F32), cos], axis=0)
    sin = jnp.concatenate([jnp.zeros((n_ctx, HEAD_DIM), F32), sin], axis=0)
    return jnp.tile(cos, (batch, 1)), jnp.tile(sin, (batch, 1))


def _config(d, n_ctx, in_w, rank, d_ff):
    heads = d // HEAD_DIM
    a_q_w = (heads // 2) * HEAD_DIM
    a_kv_w = a_q_w // Q_GROUP
    b_w = (heads // 4) * HEAD_DIM
    c_qk_w = c_v_w = (heads // 4) * HEAD_DIM
    off_bq = a_q_w + 2 * a_kv_w
    off_cq = off_bq + 3 * b_w
    off_g = off_cq + 2 * c_qk_w + c_v_w
    assert off_g + rank == in_w and off_g % rank == 0
    assert a_q_w % b_w == 0 and (a_q_w + b_w) % c_v_w == 0
    in_tn = math.gcd(math.gcd(a_kv_w, b_w), 512)
    ffn_tn = math.gcd(d_ff, 512)
    return dict(n_ctx=n_ctx, a_q_w=a_q_w, a_kv_w=a_kv_w, b_w=b_w, c_qk_w=c_qk_w, c_v_w=c_v_w,
                off_bq=off_bq, off_cq=off_cq, off_g=off_g, in_tn=in_tn, ffn_tn=ffn_tn,
                mm_tn=min(512, d))


def kernel(x, c, ctx, c_ctx, ada_down, ada_up, ada_bias, norm_mix_pre, norm_mix_post, norm_ffn_pre,
           norm_ffn_post, w_in, a_q_norm, a_k_norm, b_rel_bias, c_lambda_q1, c_lambda_k1, c_lambda_q2,
           c_lambda_k2, c_subln, w_gate_up, b_gate, w_branch, w_out, w_ffn_in, w_ffn_out):
    batch, seq, d = x.shape
    n_ctx = ctx.shape[1]
    depth, _, in_w = w_in.shape
    rank = w_gate_up.shape[1]
    d_ff = w_ffn_out.shape[1]
    t = n_ctx + seq
    m = batch * t
    cfg = _config(d, n_ctx, in_w, rank, d_ff)
    assert n_ctx % ROW_TILE == 0 and seq % ROW_TILE == 0 and seq % (NA_ROWS * GRID_W) == 0
    cfg["mm_tm"] = math.gcd(m, MM_TILE_M)
    cfg["big_tm"] = max(m // k for k in range(1, m // ROW_ALIGN + 1)
                        if m % (k * ROW_ALIGN) == 0 and m // k <= MM_TILE_M_BIG)
    geom = (t // ROW_TILE, n_ctx // ROW_TILE, batch)

    w_ffn_out_b = w_ffn_out.astype(BF16)
    b_gate_r = b_gate.reshape(depth, 1, 3 * d)

    tabs = _rope_tables(seq, n_ctx, batch, HEAD_DIM) + _rope_tables(seq, n_ctx, batch, HEAD_DIM // 2)
    plan = _nbr_plan(seq)
    nbr_bias = _nbr_bias(b_rel_bias, plan[1], HEAD_DIM ** -0.5)

    rows = -(-(batch + 1) // 8) * 8
    cond = jnp.zeros((rows, d), F32).at[:batch].set(c).at[batch].set(c_ctx)
    mods = _adaln(cond, ada_down, ada_up, ada_bias).reshape(depth, 6, rows, 1, d)

    xs, h = _embed(x, ctx, norm_mix_pre[0], mods, geom)
    for l in range(depth):
        lam_init = 0.8 - 0.6 * math.exp(-0.3 * l)
        proj = _inproj(h, w_in, l, a_q_norm[l], a_k_norm[l], tabs, cfg)
        proj3 = proj.reshape(batch, t, in_w)
        o_a = _attn_a(proj3, cfg).reshape(m, -1)
        o_b = _attn_b(proj3, nbr_bias, l, plan, cfg).reshape(m, -1)
        o_c = _attn_c(proj3, (c_lambda_q1[l], c_lambda_k1[l], c_lambda_q2[l], c_lambda_k2[l]),
                      c_subln[l], lam_init, cfg).reshape(m, -1)
        merged = _merge(proj, o_a, o_b, o_c, w_gate_up, b_gate_r, w_branch, l, cfg)
        mix = _matmul_f32(merged, w_out, l, cfg["big_tm"], cfg["mm_tn"], "out_proj", a_buffers=1)
        xs, h = _resid(xs, mix, norm_mix_post[l], norm_ffn_pre[l], mods, l, 2, l, 3, 4, geom)
        act = _ffn_in(h, w_ffn_in, l, cfg)
        f = _matmul_f32(act, w_ffn_out_b, l, min(512, cfg["mm_tm"]), cfg["mm_tn"], "ffn_out")
        if l + 1 < depth:
            xs, h = _resid(xs, f, norm_ffn_post[l], norm_mix_pre[l + 1], mods, l, 5, l + 1, 0, 1, geom)
    return _resid_out(xs, f, norm_ffn_post[depth - 1], mods, depth - 1, 5, geom).reshape(batch, seq, d)
```

```python
import functools
import math

import jax
import jax.numpy as jnp
import numpy as np
from jax import lax
from jax.experimental import pallas as pl
from jax.experimental.pallas import tpu as pltpu

HEAD_DIM = 128
GRID_W = 64
NA_ROWS = 8
NA_COLS = 16
Q_GROUP = 4
ROPE_THETA = 10000.0
EPS = 1e-6
LOG2_E = math.log2(math.e)
ROW_TILE = 256
MM_TILE_M = 1024
MM_TILE_M_BIG = 2304
SUB_TILES = 8
ROW_ALIGN = 128
VMEM_LIMIT = 58 * 1024 * 1024
BF16 = jnp.bfloat16
F32 = jnp.float32


def _params(*sem):
    return pltpu.CompilerParams(dimension_semantics=sem, vmem_limit_bytes=VMEM_LIMIT)


def _dot(a, b):
    return jnp.dot(a, b, preferred_element_type=F32)


def _dot_nt(a, b):
    return lax.dot_general(a, b, (((1,), (1,)), ((), ())), preferred_element_type=F32)


def _rms(x, gain):
    return x * lax.rsqrt(jnp.mean(x * x, axis=-1, keepdims=True) + EPS) * gain


def _adaln_kernel(cond_ref, down_ref, up_ref, bias_ref, o_ref):
    cond = cond_ref[...]
    act = (cond * jax.nn.sigmoid(cond)).astype(BF16)
    low = _dot(act, down_ref[0].astype(BF16))
    o_ref[0, 0] = _dot(low.astype(BF16), up_ref[0].astype(BF16)) + bias_ref[0]


def _adaln(cond, ada_down, ada_up, ada_bias):
    depth, d, rank = ada_down.shape
    rows = cond.shape[0]
    return pl.pallas_call(
        _adaln_kernel,
        grid=(depth, 6),
        in_specs=[
            pl.BlockSpec((rows, d), lambda l, k: (0, 0)),
            pl.BlockSpec((1, d, rank), lambda l, k: (l, 0, 0)),
            pl.BlockSpec((1, rank, d), lambda l, k: (l, 0, k)),
            pl.BlockSpec((1, 1, d), lambda l, k: (l, 0, k)),
        ],
        out_specs=pl.BlockSpec((1, 1, rows, d), lambda l, k: (l, k, 0, 0)),
        out_shape=jax.ShapeDtypeStruct((depth, 6, rows, d), F32),
        compiler_params=_params("arbitrary", "arbitrary"),
        name="adaln",
    )(cond, ada_down, ada_up, ada_bias.reshape(depth, 1, 6 * d))


def _mod_row(i, tiles_per_batch, ctx_tiles, batch):
    return jnp.where(i % tiles_per_batch < ctx_tiles, batch, i // tiles_per_batch)


def _embed_kernel(x_ref, ctx_ref, gain_ref, shift_ref, scale_ref, xs_ref, h_ref, *, tiles_per_batch, ctx_tiles):
    def emit(src):
        xs_ref[...] = src
        h_ref[...] = (_rms(src, gain_ref[...]) * (1.0 + scale_ref[0, 0, 0]) + shift_ref[0, 0, 0]).astype(BF16)

    is_ctx = pl.program_id(0) % tiles_per_batch < ctx_tiles

    @pl.when(is_ctx)
    def _():
        emit(ctx_ref[0])

    @pl.when(jnp.logical_not(is_ctx))
    def _():
        emit(x_ref[0])


def _embed(x, ctx, gain, mods, geom):
    batch, seq, d = x.shape
    tpb, ctx_tiles, _ = geom
    m = batch * tpb * ROW_TILE
    row = functools.partial(_mod_row, tiles_per_batch=tpb, ctx_tiles=ctx_tiles, batch=batch)
    tile = pl.BlockSpec((ROW_TILE, d), lambda i: (i, 0))
    kern = functools.partial(_embed_kernel, tiles_per_batch=tpb, ctx_tiles=ctx_tiles)
    return pl.pallas_call(
        kern,
        grid=(m // ROW_TILE,),
        in_specs=[
            pl.BlockSpec((1, ROW_TILE, d), lambda i: (i // tpb, jnp.maximum(i % tpb - ctx_tiles, 0), 0)),
            pl.BlockSpec((1, ROW_TILE, d), lambda i: (i // tpb, jnp.minimum(i % tpb, ctx_tiles - 1), 0)),
            pl.BlockSpec((1, d), lambda i: (0, 0)),
            pl.BlockSpec((1, 1, 1, 1, d), lambda i: (0, 0, row(i), 0, 0)),
            pl.BlockSpec((1, 1, 1, 1, d), lambda i: (0, 1, row(i), 0, 0)),
        ],
        out_specs=[tile, tile],
        out_shape=[jax.ShapeDtypeStruct((m, d), F32), jax.ShapeDtypeStruct((m, d), BF16)],
        compiler_params=_params("arbitrary"),
        name="embed",
    )(x, ctx, gain.reshape(1, d), mods, mods)


def _resid_kernel(x_ref, y_ref, post_ref, gate_ref, pre_ref, shift_ref, scale_ref, xo_ref, h_ref):
    x_new = x_ref[...] + gate_ref[0, 0, 0] * _rms(y_ref[...], post_ref[...])
    xo_ref[...] = x_new
    h = _rms(x_new, pre_ref[...])
    h_ref[...] = (h * (1.0 + scale_ref[0, 0, 0]) + shift_ref[0, 0, 0]).astype(BF16)


def _resid(xs, y, post_gain, pre_gain, mods, layer, k_gate, mod_layer, k_shift, k_scale, geom):
    m, d = xs.shape
    tpb, ctx_tiles, batch = geom
    row = functools.partial(_mod_row, tiles_per_batch=tpb, ctx_tiles=ctx_tiles, batch=batch)
    tile = pl.BlockSpec((ROW_TILE, d), lambda i: (i, 0))
    vec = pl.BlockSpec((1, d), lambda i: (0, 0))
    return pl.pallas_call(
        _resid_kernel,
        grid=(m // ROW_TILE,),
        in_specs=[
            tile, tile, vec,
            pl.BlockSpec((1, 1, 1, 1, d), lambda i: (layer, k_gate, row(i), 0, 0)),
            vec,
            pl.BlockSpec((1, 1, 1, 1, d), lambda i: (mod_layer, k_shift, row(i), 0, 0)),
            pl.BlockSpec((1, 1, 1, 1, d), lambda i: (mod_layer, k_scale, row(i), 0, 0)),
        ],
        out_specs=[tile, tile],
        out_shape=[jax.ShapeDtypeStruct((m, d), F32), jax.ShapeDtypeStruct((m, d), BF16)],
        compiler_params=_params("arbitrary"),
        name="resid",
    )(xs, y, post_gain.reshape(1, d), mods, pre_gain.reshape(1, d), mods, mods)


def _resid_out_kernel(x_ref, y_ref, post_ref, gate_ref, o_ref):
    o_ref[...] = x_ref[...] + gate_ref[0, 0, 0] * _rms(y_ref[...], post_ref[...])


def _resid_out(xs, y, post_gain, mods, layer, k_gate, geom):
    m, d = xs.shape
    tpb, ctx_tiles, batch = geom
    lat = tpb - ctx_tiles
    row = functools.partial(_mod_row, tiles_per_batch=tpb, ctx_tiles=ctx_tiles, batch=batch)
    tile = pl.BlockSpec((ROW_TILE, d), lambda i: (i, 0))
    return pl.pallas_call(
        _resid_out_kernel,
        grid=(m // ROW_TILE,),
        in_specs=[
            tile, tile, pl.BlockSpec((1, d), lambda i: (0, 0)),
            pl.BlockSpec((1, 1, 1, 1, d), lambda i: (layer, k_gate, row(i), 0, 0)),
        ],
        out_specs=pl.BlockSpec((ROW_TILE, d), lambda i: ((i // tpb) * lat + jnp.maximum(i % tpb - ctx_tiles, 0), 0)),
        out_shape=jax.ShapeDtypeStruct((batch * lat * ROW_TILE, d), F32),
        compiler_params=_params("arbitrary"),
        name="resid_out",
    )(xs, y, post_gain.reshape(1, d), mods)


def _rope(y, cos, sin):
    lane = lax.broadcasted_iota(jnp.int32, y.shape, 1)
    partner = jnp.where(lane % 2 == 0, pltpu.roll(y, HEAD_DIM - 1, 1), pltpu.roll(y, 1, 1))
    return y * cos + partner * sin


def _inproj_kernel(h_ref, w_ref, qg_ref, kg_ref, cosa_ref, sina_ref, cosc_ref, sinc_ref, o_ref,
                   *, tn, q_tiles, k_tiles, rope_c_ranges):
    j = pl.program_id(1)
    heads = tn // HEAD_DIM
    n_sub = max(k for k in range(1, SUB_TILES + 1) if h_ref.shape[0] % (16 * k) == 0)
    sub_rows = h_ref.shape[0] // n_sub

    def project(fn):
        for s in range(n_sub):
            rs = slice(s * sub_rows, (s + 1) * sub_rows)
            acc = _dot(h_ref[rs], w_ref[...].astype(BF16))
            if fn is None:
                o_ref[rs] = acc.astype(BF16)
                continue
            for hh in range(heads):
                sl = slice(hh * HEAD_DIM, (hh + 1) * HEAD_DIM)
                o_ref[rs, sl] = fn(acc[:, sl], rs).astype(BF16)

    is_q = j < q_tiles
    is_k = jnp.logical_and(j >= q_tiles, j < q_tiles + k_tiles)
    is_c = functools.reduce(jnp.logical_or,
                            [jnp.logical_and(j >= lo, j < hi) for lo, hi in rope_c_ranges])
    plain = jnp.logical_not(jnp.logical_or(jnp.logical_or(is_q, is_k), is_c))

    @pl.when(is_q)
    def _():
        project(lambda a, rs: _rope(_rms(a, qg_ref[...]), cosa_ref[rs], sina_ref[rs]))

    @pl.when(is_k)
    def _():
        project(lambda a, rs: _rope(_rms(a, kg_ref[...]), cosa_ref[rs], sina_ref[rs]))

    @pl.when(is_c)
    def _():
        project(lambda a, rs: _rope(a, cosc_ref[rs], sinc_ref[rs]))

    @pl.when(plain)
    def _():
        project(None)


def _inproj(h, w, layer, q_gain, k_gain, tabs, cfg):
    m, d = h.shape
    n = w.shape[2]
    tn, tm = cfg["in_tn"], cfg["big_tm"]
    tab = pl.BlockSpec((tm, HEAD_DIM), lambda i, j: (i, 0), pipeline_mode=pl.Buffered(1))
    vec = pl.BlockSpec((1, HEAD_DIM), lambda i, j: (0, 0))
    kern = functools.partial(
        _inproj_kernel, tn=tn, q_tiles=cfg["a_q_w"] // tn, k_tiles=cfg["a_kv_w"] // tn,
        rope_c_ranges=[(cfg["off_cq"] // tn, (cfg["off_cq"] + 2 * cfg["c_qk_w"]) // tn)])
    return pl.pallas_call(
        kern,
        grid=(m // tm, n // tn),
        in_specs=[
            pl.BlockSpec((tm, d), lambda i, j: (i, 0), pipeline_mode=pl.Buffered(1)),
            pl.BlockSpec((None, d, tn), lambda i, j: (layer, 0, j)),
            vec, vec, tab, tab, tab, tab,
        ],
        out_specs=pl.BlockSpec((tm, tn), lambda i, j: (i, j)),
        out_shape=jax.ShapeDtypeStruct((m, n), BF16),
        compiler_params=_params("arbitrary", "arbitrary"),
        name="inproj",
    )(h, w, q_gain.reshape(1, HEAD_DIM), k_gain.reshape(1, HEAD_DIM), *tabs)


ONES_ROWS = 16
EXP_CHUNK = 256


def _attend_t(q, k, vt, c):
    st = _dot_nt(k, q)
    m = jnp.max(st, axis=0, keepdims=True)
    acc = _dot(vt, jnp.exp2((st - m) * c).astype(BF16))
    d = vt.shape[0] - ONES_ROWS
    return acc[:d] / acc[d:d + 1]


def _qk_max(q, k, st_ref):
    st = _dot_nt(k, q)
    st_ref[...] = st
    return jnp.max(st, axis=0, keepdims=True)


def _exp_pv(st_ref, pt_ref, vt, m, c):
    n = st_ref.shape[0]
    for c0 in range(0, n, EXP_CHUNK):
        c1 = min(c0 + EXP_CHUNK, n)
        pt_ref[c0:c1] = jnp.exp2((st_ref[c0:c1] - m) * c).astype(BF16)
    acc = _dot(vt, pt_ref[...])
    d = vt.shape[0] - ONES_ROWS
    return acc[:d] / acc[d:d + 1]


def _attend_tiles(n_iter, n_items, load_q, store_o, k, vt, st_ref, pt_ref, m_ref, c):
    def scores(i, j):
        m_ref[j] = _qk_max(load_q(i, j), k, st_ref.at[j])

    lead = n_items // 2
    for j in range(lead):
        scores(0, j)

    def body(i, carry):
        nxt = jnp.minimum(i + 1, n_iter - 1)
        for j in range(n_items):
            if j + lead < n_items:
                scores(i, j + lead)
            else:
                scores(nxt, j + lead - n_items)
            store_o(i, j, _exp_pv(st_ref.at[j], pt_ref.at[j % 2], vt, m_ref[j], c))
        return carry

    lax.fori_loop(0, n_iter, body, 0)


def _fill_vt(vt_ref, v_ref):
    vt_ref[:HEAD_DIM] = v_ref[0].T
    vt_ref[HEAD_DIM:] = jnp.ones((ONES_ROWS, vt_ref.shape[1]), BF16)


def _attn_scratch(t, n_items):
    return [pltpu.VMEM((HEAD_DIM + ONES_ROWS, t), BF16), pltpu.VMEM((n_items, t, ROW_TILE), F32),
            pltpu.VMEM((2, t, ROW_TILE), BF16), pltpu.VMEM((n_items, 1, ROW_TILE), F32)]


def _attn_a_kernel(q_ref, k_ref, v_ref, o_ref, vt_ref, st_ref, pt_ref, m_ref, *, n_ctx, group, c):
    t = k_ref.shape[1]
    tq = ROW_TILE
    _fill_vt(vt_ref, v_ref)
    heads = [slice(g * HEAD_DIM, (g + 1) * HEAD_DIM) for g in range(group)]

    for r0 in range(0, n_ctx, tq):
        for sl in heads:
            o = _attend_t(q_ref[0, r0:r0 + tq, sl], k_ref[0, :n_ctx], vt_ref[:, :n_ctx], c)
            o_ref[0, r0:r0 + tq, sl] = o.T.astype(BF16)

    def rows(i):
        return pl.ds(pl.multiple_of(n_ctx + i * tq, tq), tq)

    def load_q(i, g):
        return q_ref[0, rows(i), heads[g]]

    def store_o(i, g, o):
        o_ref[0, rows(i), heads[g]] = o.T.astype(BF16)

    _attend_tiles((t - n_ctx) // tq, group, load_q, store_o, k_ref[0], vt_ref[...], st_ref, pt_ref, m_ref, c)


def _attn_a(proj, cfg):
    b, t, _ = proj.shape
    gw = Q_GROUP * HEAD_DIM
    kvh = cfg["a_kv_w"] // HEAD_DIM
    k_blk = cfg["a_q_w"] // HEAD_DIM
    v_blk = (cfg["a_q_w"] + cfg["a_kv_w"]) // HEAD_DIM
    kern = functools.partial(_attn_a_kernel, n_ctx=cfg["n_ctx"], group=Q_GROUP, c=HEAD_DIM ** -0.5 * LOG2_E)
    return pl.pallas_call(
        kern,
        grid=(b, kvh),
        in_specs=[
            pl.BlockSpec((1, t, gw), lambda bi, hi: (bi, 0, hi)),
            pl.BlockSpec((1, t, HEAD_DIM), lambda bi, hi: (bi, 0, k_blk + hi)),
            pl.BlockSpec((1, t, HEAD_DIM), lambda bi, hi: (bi, 0, v_blk + hi)),
        ],
        out_specs=pl.BlockSpec((1, t, gw), lambda bi, hi: (bi, 0, hi)),
        out_shape=jax.ShapeDtypeStruct((b, t, cfg["a_q_w"]), BF16),
        scratch_shapes=_attn_scratch(t, Q_GROUP),
        compiler_params=_params("arbitrary", "arbitrary"),
        name="attn_a",
    )(proj, proj, proj)


def _attn_c_kernel(q_ref, k_ref, v_ref, l1_ref, l2_ref, l3_ref, l4_ref, gain_ref, o_ref,
                   vt_ref, st_ref, pt_ref, m_ref, *, n_ctx, c, lam_init):
    t = k_ref.shape[1]
    tq = ROW_TILE
    _fill_vt(vt_ref, v_ref)
    lam = (jnp.exp(jnp.sum(l1_ref[...] * l2_ref[...], axis=-1, keepdims=True))
           - jnp.exp(jnp.sum(l3_ref[...] * l4_ref[...], axis=-1, keepdims=True)) + lam_init)
    lane = lax.broadcasted_iota(jnp.int32, (tq, HEAD_DIM), 1)
    first = lane < HEAD_DIM // 2

    def component(q, comp):
        return jnp.where(first == (comp == 0), q, jnp.zeros_like(q))

    def finish(o1, o2):
        o = (o1 - lam * o2).T
        return (_rms(o, gain_ref[...]) * (1.0 - lam_init)).astype(BF16)

    for r0 in range(0, n_ctx, tq):
        q = q_ref[0, r0:r0 + tq]
        o1, o2 = [_attend_t(component(q, comp), k_ref[0, :n_ctx], vt_ref[:, :n_ctx], c) for comp in range(2)]
        o_ref[0, r0:r0 + tq] = finish(o1, o2)

    def rows(i, j):
        return pl.ds(pl.multiple_of(n_ctx + (2 * i + j // 2) * tq, tq), tq)

    def load_q(i, j):
        return component(q_ref[0, rows(i, j)], j % 2)

    pending = {}

    def store_o(i, j, o):
        if j % 2 == 0:
            pending[0] = o
        else:
            o_ref[0, rows(i, j)] = finish(pending.pop(0), o)

    _attend_tiles((t - n_ctx) // (2 * tq), 4, load_q, store_o, k_ref[0], vt_ref[...], st_ref, pt_ref, m_ref, c)


def _attn_c(proj, lams, subln, lam_init, cfg):
    b, t, _ = proj.shape
    heads = cfg["c_v_w"] // HEAD_DIM
    q_blk = cfg["off_cq"] // HEAD_DIM
    k_blk = q_blk + cfg["c_qk_w"] // HEAD_DIM
    v_blk = k_blk + cfg["c_qk_w"] // HEAD_DIM
    qk_dim = HEAD_DIM // 2
    lam_spec = pl.BlockSpec((1, qk_dim), lambda bi, hi: (0, 0))
    kern = functools.partial(_attn_c_kernel, n_ctx=cfg["n_ctx"], c=qk_dim ** -0.5 * LOG2_E, lam_init=lam_init)
    return pl.pallas_call(
        kern,
        grid=(b, heads),
        in_specs=[
            pl.BlockSpec((1, t, HEAD_DIM), lambda bi, hi: (bi, 0, q_blk + hi)),
            pl.BlockSpec((1, t, HEAD_DIM), lambda bi, hi: (bi, 0, k_blk + hi)),
            pl.BlockSpec((1, t, HEAD_DIM), lambda bi, hi: (bi, 0, v_blk + hi)),
            lam_spec, lam_spec, lam_spec, lam_spec,
            pl.BlockSpec((1, HEAD_DIM), lambda bi, hi: (0, 0)),
        ],
        out_specs=pl.BlockSpec((1, t, HEAD_DIM), lambda bi, hi: (bi, 0, hi)),
        out_shape=jax.ShapeDtypeStruct((b, t, cfg["c_v_w"]), BF16),
        scratch_shapes=_attn_scratch(t, 4),
        compiler_params=_params("arbitrary", "arbitrary"),
        name="attn_c",
    )(proj, proj, proj, *[v.reshape(1, qk_dim) for v in lams], subln.reshape(1, HEAD_DIM))


def _attn_b_kernel(q_ref, k_ref, v_ref, pair_ref, o_ref, bias_ref, v1_ref, *, n_ctx, blocks, pair_idx, q_rows,
                   k_rows, c):
    @pl.when(pl.program_id(1) == 0)
    def _():
        for case, case_idx in enumerate(pair_idx):
            for a, row_idx in enumerate(case_idx):
                for p, pair in enumerate(row_idx):
                    bias_ref[case, a * GRID_W:(a + 1) * GRID_W, 2 * p * GRID_W:2 * (p + 1) * GRID_W] = (
                        pair_ref[0, pair])

    v1_ref[:, :HEAD_DIM] = v_ref[0]
    v1_ref[:, HEAD_DIM:] = jnp.ones((v1_ref.shape[0], HEAD_DIM), BF16)
    k_ctx = k_ref[0, :n_ctx]
    v_ctx = v1_ref[:n_ctx]

    def finish(acc):
        return (acc[:, :HEAD_DIM] / acc[:, HEAD_DIM:]).astype(BF16)

    s = _dot_nt(q_ref[0, :n_ctx], k_ctx)
    p = jnp.exp2((s - jnp.max(s, axis=-1, keepdims=True)) * c)
    o_ref[0, :n_ctx] = finish(_dot(p.astype(BF16), v_ctx))
    for q0, k0, case in blocks:
        q = q_ref[0, n_ctx + q0:n_ctx + q0 + q_rows]
        win = slice(n_ctx + k0, n_ctx + k0 + k_rows)
        s_loc = _dot_nt(q, k_ref[0, win]) + bias_ref[case]
        s_ctx = _dot_nt(q, k_ctx)
        m = jnp.maximum(jnp.max(s_loc, axis=-1, keepdims=True), jnp.max(s_ctx, axis=-1, keepdims=True))
        acc = (_dot(jnp.exp2((s_loc - m) * c).astype(BF16), v1_ref[win])
               + _dot(jnp.exp2((s_ctx - m) * c).astype(BF16), v_ctx))
        o_ref[0, n_ctx + q0:n_ctx + q0 + q_rows] = finish(acc)


def _nbr_plan(seq):
    rows = seq // GRID_W
    kr = min(NA_ROWS, rows)
    q_rows_n = min(NA_ROWS, rows)
    win = min(2 * NA_ROWS, rows)
    cases, blocks = [], []
    for i in range(rows // q_rows_n):
        ws = int(np.clip(i * q_rows_n - kr // 2, 0, rows - win))
        drmap = np.full((q_rows_n, win), -1, np.int64)
        for a in range(q_rows_n):
            r = i * q_rows_n + a
            r0 = int(np.clip(r - kr // 2, 0, rows - kr))
            for w in range(win):
                kr_abs = ws + w
                if r0 <= kr_abs < r0 + kr:
                    drmap[a, w] = kr_abs - r + (NA_ROWS - 1)
        key = drmap.tobytes()
        keys = [c.tobytes() for c in cases]
        if key not in keys:
            cases.append(drmap)
            keys.append(key)
        blocks.append((i * q_rows_n * GRID_W, ws * GRID_W, keys.index(key)))
    masked = 2 * NA_ROWS - 1
    case_maps = np.where(np.stack(cases) < 0, masked, np.stack(cases))
    pair_list = sorted({(int(d0), int(d1)) for d0, d1 in case_maps.reshape(-1, 2)})
    pair_idx = [[[pair_list.index((int(row[2 * p]), int(row[2 * p + 1]))) for p in range(win // 2)]
                 for row in case] for case in case_maps]
    return blocks, np.asarray(pair_list), pair_idx, q_rows_n * GRID_W, win * GRID_W


def _nbr_bias(rpb, pairs, scale):
    depth, heads = rpb.shape[:2]
    c = np.arange(GRID_W)
    c0 = np.clip(c - NA_COLS // 2, 0, GRID_W - NA_COLS)
    col_mask = (c[None, :] >= c0[:, None]) & (c[None, :] < c0[:, None] + NA_COLS)
    col_idx = np.clip(c[None, :] - c[:, None] + NA_COLS - 1, 0, 2 * NA_COLS - 2)
    toe = jnp.where(col_mask, rpb.astype(F32)[:, :, :, col_idx] / scale, -jnp.inf)
    toe = jnp.concatenate([toe, jnp.full((depth, heads, 1, GRID_W, GRID_W), -jnp.inf, F32)], axis=2)
    return jnp.concatenate([jnp.take(toe, pairs[:, 0], axis=2), jnp.take(toe, pairs[:, 1], axis=2)], axis=-1)


def _attn_b(proj, pair_bias, layer, plan, cfg):
    b, t, _ = proj.shape
    heads = cfg["b_w"] // HEAD_DIM
    q_blk = cfg["off_bq"] // HEAD_DIM
    k_blk = q_blk + heads
    v_blk = k_blk + heads
    blocks, _, pair_idx, q_rows, k_rows = plan
    n_pair = pair_bias.shape[2]
    kern = functools.partial(_attn_b_kernel, n_ctx=cfg["n_ctx"], blocks=blocks, pair_idx=pair_idx,
                             q_rows=q_rows, k_rows=k_rows, c=HEAD_DIM ** -0.5 * LOG2_E)
    return pl.pallas_call(
        kern,
        grid=(heads, b),
        in_specs=[
            pl.BlockSpec((1, t, HEAD_DIM), lambda hi, bi: (bi, 0, q_blk + hi)),
            pl.BlockSpec((1, t, HEAD_DIM), lambda hi, bi: (bi, 0, k_blk + hi)),
            pl.BlockSpec((1, t, HEAD_DIM), lambda hi, bi: (bi, 0, v_blk + hi)),
            pl.BlockSpec((None, 1, n_pair, GRID_W, 2 * GRID_W), lambda hi, bi: (layer, hi, 0, 0, 0)),
        ],
        out_specs=pl.BlockSpec((1, t, HEAD_DIM), lambda hi, bi: (bi, 0, hi)),
        out_shape=jax.ShapeDtypeStruct((b, t, cfg["b_w"]), BF16),
        scratch_shapes=[pltpu.VMEM((len(pair_idx), q_rows, k_rows), F32), pltpu.VMEM((t, 2 * HEAD_DIM), BF16)],
        compiler_params=_params("arbitrary", "arbitrary"),
        name="attn_b",
    )(proj, proj, proj, pair_bias)


def _merge_kernel(g_ref, oa_ref, ob_ref, oc_ref, wga_ref, wgb_ref, wgc_ref, bga_ref, bgb_ref, bgc_ref,
                  wa_ref, wb_ref, wc_ref, o_ref):
    g_low = g_ref[...]
    acc = None
    for o_br, w_br, wg, bg in ((oa_ref, wa_ref, wga_ref, bga_ref), (ob_ref, wb_ref, wgb_ref, bgb_ref),
                               (oc_ref, wc_ref, wgc_ref, bgc_ref)):
        gate = jax.nn.sigmoid(_dot(g_low, wg[...].astype(BF16)) + bg[...])
        term = gate * _dot(o_br[...], w_br[...].astype(BF16))
        acc = term if acc is None else acc + term
    o_ref[...] = acc.astype(BF16)


def _merge(proj2, o_a, o_b, o_c, w_gate, b_gate, w_branch, layer, cfg):
    m = proj2.shape[0]
    d = w_branch.shape[2]
    rank = w_gate.shape[1]
    tm, tn = cfg["mm_tm"], cfg["mm_tn"]
    a_w, b_w, c_w = cfg["a_q_w"], cfg["b_w"], cfg["c_v_w"]
    g_blk = cfg["off_g"] // rank
    nb = d // tn
    gate_specs = [pl.BlockSpec((None, rank, tn), lambda i, j, br=br: (layer, 0, br * nb + j)) for br in range(3)]
    bias_specs = [pl.BlockSpec((None, 1, tn), lambda i, j, br=br: (layer, 0, br * nb + j)) for br in range(3)]
    return pl.pallas_call(
        _merge_kernel,
        grid=(m // tm, nb),
        in_specs=[
            pl.BlockSpec((tm, rank), lambda i, j: (i, g_blk)),
            pl.BlockSpec((tm, a_w), lambda i, j: (i, 0)),
            pl.BlockSpec((tm, b_w), lambda i, j: (i, 0)),
            pl.BlockSpec((tm, c_w), lambda i, j: (i, 0)),
            *gate_specs, *bias_specs,
            pl.BlockSpec((None, a_w, tn), lambda i, j: (layer, 0, j)),
            pl.BlockSpec((None, b_w, tn), lambda i, j: (layer, a_w // b_w, j)),
            pl.BlockSpec((None, c_w, tn), lambda i, j: (layer, (a_w + b_w) // c_w, j)),
        ],
        out_specs=pl.BlockSpec((tm, tn), lambda i, j: (i, j)),
        out_shape=jax.ShapeDtypeStruct((m, d), BF16),
        compiler_params=_params("arbitrary", "arbitrary"),
        name="merge",
    )(proj2, o_a, o_b, o_c, w_gate, w_gate, w_gate, b_gate, b_gate, b_gate, w_branch, w_branch, w_branch)


def _matmul_kernel(a_ref, w_ref, o_ref):
    w = w_ref[...]
    o_ref[...] = _dot(a_ref[...], w if w.dtype == BF16 else w.astype(BF16))


def _matmul_f32(a, w, layer, tm, tn, name, weights_outer=False):
    m, k = a.shape
    n = w.shape[2]
    if weights_outer:
        grid, row, col = (n // tn, m // tm), (lambda j, i: i), (lambda j, i: j)
    else:
        grid, row, col = (m // tm, n // tn), (lambda i, j: i), (lambda i, j: j)
    return pl.pallas_call(
        _matmul_kernel,
        grid=grid,
        in_specs=[pl.BlockSpec((tm, k), lambda p, q: (row(p, q), 0)),
                  pl.BlockSpec((None, k, tn), lambda p, q: (layer, 0, col(p, q)))],
        out_specs=pl.BlockSpec((tm, tn), lambda p, q: (row(p, q), col(p, q))),
        out_shape=jax.ShapeDtypeStruct((m, n), F32),
        compiler_params=_params("arbitrary", "arbitrary"),
        name=name,
    )(a, w)


def _ffn_in_kernel(h_ref, wg_ref, wu_ref, o_ref):
    h = h_ref[...]
    gate = _dot(h, wg_ref[...].astype(BF16))
    up = _dot(h, wu_ref[...].astype(BF16))
    o_ref[...] = (gate * jax.nn.sigmoid(gate) * up).astype(BF16)


def _ffn_in(h, w, layer, cfg):
    m, d = h.shape
    d_ff = w.shape[2] // 2
    tm, tn = cfg["big_tm"], cfg["ffn_tn"]
    up_blk = d_ff // tn
    return pl.pallas_call(
        _ffn_in_kernel,
        grid=(m // tm, d_ff // tn),
        in_specs=[pl.BlockSpec((tm, d), lambda i, j: (i, 0)),
                  pl.BlockSpec((None, d, tn), lambda i, j: (layer, 0, j)),
                  pl.BlockSpec((None, d, tn), lambda i, j: (layer, 0, up_blk + j))],
        out_specs=pl.BlockSpec((tm, tn), lambda i, j: (i, j)),
        out_shape=jax.ShapeDtypeStruct((m, d_ff), BF16),
        compiler_params=_params("arbitrary", "arbitrary"),
        name="ffn_in",
    )(h, w, w)


def _rope_tables(seq, n_ctx, batch, dim):
    t = np.arange(seq)
    rows = (t // GRID_W).astype(np.float32)
    cols = (t % GRID_W).astype(np.float32)
    n_pairs = dim // 4
    freqs = jnp.asarray(ROPE_THETA, F32) ** (-jnp.arange(n_pairs, dtype=F32) / n_pairs)
    ang = jnp.concatenate([jnp.asarray(rows)[:, None] * freqs, jnp.asarray(cols)[:, None] * freqs], axis=-1)
    sign = jnp.asarray(np.tile([-1.0, 1.0], dim // 2), F32)
    cos = jnp.tile(jnp.repeat(jnp.cos(ang), 2, axis=1), (1, HEAD_DIM // dim))
    sin = jnp.tile(jnp.repeat(jnp.sin(ang), 2, axis=1) * sign, (1, HEAD_DIM // dim))
    cos = jnp.concatenate([jnp.ones((n_ctx, HEAD_DIM), F32), cos], axis=0)
    sin = jnp.concatenate([jnp.zeros((n_ctx, HEAD_DIM), F32), sin], axis=0)
    return jnp.tile(cos, (batch, 1)), jnp.tile(sin, (batch, 1))


def _config(d, n_ctx, in_w, rank, d_ff):
    heads = d // HEAD_DIM
    a_q_w = (heads // 2) * HEAD_DIM
    a_kv_w = a_q_w // Q_GROUP
    b_w = (heads // 4) * HEAD_DIM
    c_qk_w = c_v_w = (heads // 4) * HEAD_DIM
    off_bq = a_q_w + 2 * a_kv_w
    off_cq = off_bq + 3 * b_w
    off_g = off_cq + 2 * c_qk_w + c_v_w
    assert off_g + rank == in_w and off_g % rank == 0
    assert a_q_w % b_w == 0 and (a_q_w + b_w) % c_v_w == 0
    in_tn = math.gcd(math.gcd(a_kv_w, b_w), 512)
    ffn_tn = math.gcd(d_ff, 512)
    return dict(n_ctx=n_ctx, a_q_w=a_q_w, a_kv_w=a_kv_w, b_w=b_w, c_qk_w=c_qk_w, c_v_w=c_v_w,
                off_bq=off_bq, off_cq=off_cq, off_g=off_g, in_tn=in_tn, ffn_tn=ffn_tn,
                mm_tn=min(512, d))


def kernel(x, c, ctx, c_ctx, ada_down, ada_up, ada_bias, norm_mix_pre, norm_mix_post, norm_ffn_pre,
           norm_ffn_post, w_in, a_q_norm, a_k_norm, b_rel_bias, c_lambda_q1, c_lambda_k1, c_lambda_q2,
           c_lambda_k2, c_subln, w_gate_up, b_gate, w_branch, w_out, w_ffn_in, w_ffn_out):
    batch, seq, d = x.shape
    n_ctx = ctx.shape[1]
    depth, _, in_w = w_in.shape
    rank = w_gate_up.shape[1]
    d_ff = w_ffn_out.shape[1]
    t = n_ctx + seq
    m = batch * t
    cfg = _config(d, n_ctx, in_w, rank, d_ff)
    assert n_ctx % ROW_TILE == 0 and seq % ROW_TILE == 0 and seq % (NA_ROWS * GRID_W) == 0
    cfg["mm_tm"] = math.gcd(m, MM_TILE_M)
    cfg["big_tm"] = max(m // k for k in range(1, m // ROW_ALIGN + 1)
                        if m % (k * ROW_ALIGN) == 0 and m // k <= MM_TILE_M_BIG)
    geom = (t // ROW_TILE, n_ctx // ROW_TILE, batch)

    w_ffn_out_b = w_ffn_out.astype(BF16)
    b_gate_r = b_gate.reshape(depth, 1, 3 * d)

    tabs = _rope_tables(seq, n_ctx, batch, HEAD_DIM) + _rope_tables(seq, n_ctx, batch, HEAD_DIM // 2)
    plan = _nbr_plan(seq)
    nbr_bias = _nbr_bias(b_rel_bias, plan[1], HEAD_DIM ** -0.5)

    rows = -(-(batch + 1) // 8) * 8
    cond = jnp.zeros((rows, d), F32).at[:batch].set(c).at[batch].set(c_ctx)
    mods = _adaln(cond, ada_down, ada_up, ada_bias).reshape(depth, 6, rows, 1, d)

    xs, h = _embed(x, ctx, norm_mix_pre[0], mods, geom)
    for l in range(depth):
        lam_init = 0.8 - 0.6 * math.exp(-0.3 * l)
        proj = _inproj(h, w_in, l, a_q_norm[l], a_k_norm[l], tabs, cfg)
        proj3 = proj.reshape(batch, t, in_w)
        o_a = _attn_a(proj3, cfg).reshape(m, -1)
        o_b = _attn_b(proj3, nbr_bias, l, plan, cfg).reshape(m, -1)
        o_c = _attn_c(proj3, (c_lambda_q1[l], c_lambda_k1[l], c_lambda_q2[l], c_lambda_k2[l]),
                      c_subln[l], lam_init, cfg).reshape(m, -1)
        merged = _merge(proj, o_a, o_b, o_c, w_gate_up, b_gate_r, w_branch, l, cfg)
        mix = _matmul_f32(merged, w_out, l, cfg["mm_tm"], cfg["mm_tn"], "out_proj", weights_outer=True)
        xs, h = _resid(xs, mix, norm_mix_post[l], norm_ffn_pre[l], mods, l, 2, l, 3, 4, geom)
        act = _ffn_in(h, w_ffn_in, l, cfg)
        f = _matmul_f32(act, w_ffn_out_b, l, min(512, cfg["mm_tm"]), cfg["mm_tn"], "ffn_out")
        if l + 1 < depth:
            xs, h = _resid(xs, f, norm_ffn_post[l], norm_mix_pre[l + 1], mods, l, 5, l + 1, 0, 1, geom)
    return _resid_out(xs, f, norm_ffn_post[depth - 1], mods, depth - 1, 5, geom).reshape(batch, seq, d)
```
